```python
import math
import jax, jax.numpy as jnp
from jax import lax
import numpy as np

D_MODEL = 1024
BATCH = 8
SEQ = 8192
DEPTH = 1

HEAD_DIM = 64
SWA_Q_HEADS = 8
SWA_KV_HEADS = 2
SWA_GROUP = SWA_Q_HEADS // SWA_KV_HEADS
SWA_WINDOW = 128
SB_HEADS = 8
BLOCK = 128
REL_BUCKETS = 32
REL_MAX_DIST = 128
D_FF = 2816
N_BRANCH = 2
RMS_EPS = 1e-6
NEG_BIG = -1e30

SWA_Q_W = SWA_Q_HEADS * HEAD_DIM
SWA_KV_W = SWA_KV_HEADS * HEAD_DIM
SB_W = SB_HEADS * HEAD_DIM
IN_SIZES = (SWA_Q_W, SWA_KV_W, SWA_KV_W, SB_W, SB_W, SB_W, D_MODEL, D_MODEL)
IN_W = sum(IN_SIZES)
IN_SPLITS = tuple(int(v) for v in np.cumsum(IN_SIZES)[:-1])

kernel_name = 'hybrid_swa_sink_stickbreaking_macaron'


def rmsnorm(x, g):
    xf = x.astype(jnp.float32)
    y = xf * lax.rsqrt(jnp.mean(xf * xf, axis=-1, keepdims=True) + RMS_EPS) * g.astype(jnp.float32)
    return y.astype(x.dtype)


def swiglu(h, w1, w3, w2):
    return (jax.nn.silu(h @ w1) * (h @ w3)) @ w2


def rel_bucket(dist):
    max_exact = REL_BUCKETS // 2
    d = jnp.maximum(dist, 1).astype(jnp.float32)
    large = max_exact + (jnp.log(d / max_exact) / math.log(REL_MAX_DIST / max_exact)
                         * (REL_BUCKETS - max_exact)).astype(jnp.int32)
    large = jnp.minimum(large, REL_BUCKETS - 1)
    return jnp.where(dist < max_exact, dist, large)


def sliding_window_attention(q, k, v, sinks, rel_table):
    B, S = q.shape[0], q.shape[1]
    nb = S // BLOCK
    qb = q.astype(jnp.float32).reshape(B, nb, BLOCK, SWA_KV_HEADS, SWA_GROUP, HEAD_DIM)
    kb = k.astype(jnp.float32).reshape(B, nb, BLOCK, SWA_KV_HEADS, HEAD_DIM)
    vb = v.astype(jnp.float32).reshape(B, nb, BLOCK, SWA_KV_HEADS, HEAD_DIM)
    pad = ((0, 0), (1, 0), (0, 0), (0, 0), (0, 0))
    kw = jnp.concatenate([jnp.pad(kb, pad)[:, :-1], kb], axis=2)
    vw = jnp.concatenate([jnp.pad(vb, pad)[:, :-1], vb], axis=2)
    logits = jnp.einsum('bnqhgd,bnkhd->bnhgqk', qb, kw) * (HEAD_DIM ** -0.5)
    qi = jnp.arange(BLOCK)[:, None] + BLOCK
    kj = jnp.arange(2 * BLOCK)[None, :]
    dist = qi - kj
    band = (dist >= 0) & (dist < SWA_WINDOW)
    bias = rel_table.astype(jnp.float32)[rel_bucket(jnp.maximum(dist, 0))]
    bias = bias.transpose(2, 0, 1).reshape(SWA_KV_HEADS, SWA_GROUP, BLOCK, 2 * BLOCK)
    key_pos = jnp.arange(nb)[:, None] * BLOCK + jnp.arange(2 * BLOCK)[None, :] - BLOCK
    valid = band[None] & (key_pos >= 0)[:, None, :]
    logits = jnp.where(valid[None, :, None, None], logits + bias, NEG_BIG)
    sink = sinks.astype(jnp.float32).reshape(SWA_KV_HEADS, SWA_GROUP)[None, None, :, :, None, None]
    m = jnp.maximum(jnp.max(logits, axis=-1, keepdims=True), sink)
    p = jnp.exp(logits - m)
    p = p / (jnp.sum(p, axis=-1, keepdims=True) + jnp.exp(sink - m))
    o = jnp.einsum('bnhgqk,bnkhd->bnqhgd', p, vw)
    return o.reshape(B, S, SWA_Q_W).astype(q.dtype)


def stick_breaking_attention(q, k, v):
    B, S = q.shape[0], q.shape[1]
    nb = S // BLOCK
    qf = q.astype(jnp.float32).transpose(0, 2, 1, 3) * (HEAD_DIM ** -0.5)
    kf = k.astype(jnp.float32).transpose(0, 2, 1, 3)
    vf = v.astype(jnp.float32).transpose(0, 2, 1, 3)
    qblocks = qf.reshape(B, SB_HEADS, nb, BLOCK, HEAD_DIM).transpose(2, 0, 1, 3, 4)
    key_pos = jnp.arange(S)

    def one_block(args):
        q_blk, start = args
        z = jnp.einsum('bhqd,bhkd->bhqk', q_blk, kf)
        q_pos = start + jnp.arange(BLOCK)
        causal = key_pos[None, :] < q_pos[:, None]
        log_keep = jnp.where(causal, jax.nn.log_sigmoid(-z), 0.0)
        rev = lax.cumsum(log_keep, axis=3, reverse=True)
        between = jnp.concatenate([rev[..., 1:], jnp.zeros_like(rev[..., :1])], axis=-1)
        a = jnp.where(causal, jnp.exp(jax.nn.log_sigmoid(z) + between), 0.0)
        return jnp.einsum('bhqk,bhkd->bhqd', a, vf)

    o = lax.map(one_block, (qblocks, jnp.arange(nb) * BLOCK))
    return o.transpose(1, 0, 3, 2, 4).reshape(B, S, SB_W).astype(q.dtype)


def setup_inputs(seed: int = 0) -> dict:
    key = jax.random.key(seed)
    ks = jax.random.split(key, 20)
    f32 = jnp.float32

    def w(k, shape, fan_in):
        return jax.random.normal(k, shape, f32) * (fan_in ** -0.5)

    def gain(k):
        return 1.0 + 0.02 * jax.random.normal(k, (DEPTH, D_MODEL), f32)

    return {
        'x': jax.random.normal(ks[0], (BATCH, SEQ, D_MODEL), f32),
        'norm_ffn1': gain(ks[1]),
        'ffn1_w1': w(ks[2], (DEPTH, D_MODEL, D_FF), D_MODEL),
        'ffn1_w3': w(ks[3], (DEPTH, D_MODEL, D_FF), D_MODEL),
        'ffn1_w2': w(ks[4], (DEPTH, D_FF, D_MODEL), D_FF),
        'norm_mix': gain(ks[5]),
        'w_in': w(ks[6], (DEPTH, D_MODEL, IN_W), D_MODEL),
        'swa_sinks': 0.5 * jax.random.normal(ks[7], (DEPTH, SWA_Q_HEADS), f32),
        'rel_bias': 0.5 * jax.random.normal(ks[8], (REL_BUCKETS, SWA_Q_HEADS), f32),
        'w_branch_swa': w(ks[9], (DEPTH, SWA_Q_W, D_MODEL), SWA_Q_W),
        'w_branch_sb': w(ks[10], (DEPTH, SB_W, D_MODEL), SB_W),
        'w_out': w(ks[11], (DEPTH, D_MODEL, D_MODEL), D_MODEL),
        'norm_ffn2': gain(ks[12]),
        'ffn2_w1': w(ks[13], (DEPTH, D_MODEL, D_FF), D_MODEL),
        'ffn2_w3': w(ks[14], (DEPTH, D_MODEL, D_FF), D_MODEL),
        'ffn2_w2': w(ks[15], (DEPTH, D_FF, D_MODEL), D_FF),
        'norm_final': 1.0 + 0.02 * jax.random.normal(ks[16], (D_MODEL,), f32),
    }


def reference(x, norm_ffn1, ffn1_w1, ffn1_w3, ffn1_w2, norm_mix, w_in, swa_sinks, rel_bias,
              w_branch_swa, w_branch_sb, w_out, norm_ffn2, ffn2_w1, ffn2_w3, ffn2_w2, norm_final):
    B, S = x.shape[0], x.shape[1]
    for layer in range(DEPTH):
        h = rmsnorm(x, norm_ffn1[layer])
        x = x + 0.5 * swiglu(h, ffn1_w1[layer], ffn1_w3[layer], ffn1_w2[layer])
        h = rmsnorm(x, norm_mix[layer])
        proj = h @ w_in[layer]
        q_a, k_a, v_a, q_b, k_b, v_b, g_a, g_b = jnp.split(proj, IN_SPLITS, axis=-1)
        o_a = sliding_window_attention(
            q_a.reshape(B, S, SWA_Q_HEADS, HEAD_DIM),
            k_a.reshape(B, S, SWA_KV_HEADS, HEAD_DIM),
            v_a.reshape(B, S, SWA_KV_HEADS, HEAD_DIM),
            swa_sinks[layer], rel_bias)
        o_b = stick_breaking_attention(
            q_b.reshape(B, S, SB_HEADS, HEAD_DIM),
            k_b.reshape(B, S, SB_HEADS, HEAD_DIM),
            v_b.reshape(B, S, SB_HEADS, HEAD_DIM))
        merged = (jax.nn.sigmoid(g_a) * (o_a @ w_branch_swa[layer])
                  + jax.nn.sigmoid(g_b) * (o_b @ w_branch_sb[layer]))
        x = x + merged @ w_out[layer]
        h = rmsnorm(x, norm_ffn2[layer])
        x = x + 0.5 * swiglu(h, ffn2_w1[layer], ffn2_w3[layer], ffn2_w2[layer])
    return rmsnorm(x, norm_final)
```

```python
import functools
import math

import jax
import jax.numpy as jnp
import numpy as np
from jax import lax
from jax.experimental import pallas as pl
from jax.experimental.pallas import tpu as pltpu

F32 = jnp.float32
BF16 = jnp.bfloat16

HEAD_DIM = 64
SWA_Q_HEADS = 8
SWA_KV_HEADS = 2
SWA_WINDOW = 128
SB_HEADS = 8
BLOCK = 128
REL_BUCKETS = 32
REL_MAX_DIST = 128
RMS_EPS = 1e-6
NEG_BIG = -1e30

LANES = 128
HEADS_PER_TILE = LANES // HEAD_DIM
FFN_CHUNK = 256
PROJ_CHUNK = 256
TOKEN_TILE = 512
SB_TQ = 128
SB_TK = 256
SB_LOG_UNDERFLOW = -104.0
SWA_STEP = 1024
VMEM_LIMIT = 60000 * 1024


def _rms(x, g):
    ms = jnp.mean(x * x, axis=-1, keepdims=True)
    return x * lax.rsqrt(ms + RMS_EPS) * g


def _dot(a, b):
    return jnp.dot(a, b, preferred_element_type=F32)


def _dot_nt(a, b):
    return lax.dot_general(a, b, (((1,), (1,)), ((), ())), preferred_element_type=F32)


def _resident(shape):
    nd = len(shape)
    return pl.BlockSpec(shape, lambda *_: (0,) * nd, pipeline_mode=pl.Buffered(1))


def _ffn_kernel(x_ref, g_ref, w1_ref, w3_ref, w2_ref, gf_ref, o_ref, acc_ref, *, final_norm):
    x = x_ref[...]
    h = _rms(x, g_ref[...]).astype(BF16)
    acc_ref[...] = jnp.zeros_like(acc_ref)

    def chunk(c, carry):
        a = _dot(h, w1_ref[c])
        b = _dot(h, w3_ref[c])
        gated = (a * jax.nn.sigmoid(a) * b).astype(BF16)
        acc_ref[...] += _dot(gated, w2_ref[c])
        return carry

    lax.fori_loop(0, w1_ref.shape[0], chunk, 0)
    y = x + 0.5 * acc_ref[...]
    if final_norm:
        y = _rms(y, gf_ref[...])
    o_ref[...] = y


def _ffn(x, gain, w1c, w3c, w2c, gain_final, *, final_norm):
    t, d = x.shape
    nch, _, fc = w1c.shape
    tm = TOKEN_TILE
    return pl.pallas_call(
        functools.partial(_ffn_kernel, final_norm=final_norm),
        grid=(t // tm,),
        in_specs=[
            pl.BlockSpec((tm, d), lambda i: (i, 0)),
            _resident((1, d)),
            _resident((nch, d, fc)),
            _resident((nch, d, fc)),
            _resident((nch, fc, d)),
            _resident((1, d)),
        ],
        out_specs=pl.BlockSpec((tm, d), lambda i: (i, 0)),
        out_shape=jax.ShapeDtypeStruct((t, d), F32),
        scratch_shapes=[pltpu.VMEM((tm, d), F32)],
        compiler_params=pltpu.CompilerParams(
            dimension_semantics=("arbitrary",), vmem_limit_bytes=VMEM_LIMIT),
        name="ffn_final" if final_norm else "ffn",
    )(x, gain, w1c, w3c, w2c, gain_final)


def _proj_kernel(x_ref, g_ref, wqkv_ref, wg_ref, qkv_ref, gate_ref):
    h = _rms(x_ref[...], g_ref[...]).astype(BF16)
    for c in range(wqkv_ref.shape[0]):
        qkv_ref[:, c * PROJ_CHUNK:(c + 1) * PROJ_CHUNK] = _dot(h, wqkv_ref[c]).astype(BF16)
    for c in range(wg_ref.shape[0]):
        gate_ref[:, c * PROJ_CHUNK:(c + 1) * PROJ_CHUNK] = _dot(h, wg_ref[c])


def _proj(x, gain, wqkv, wg):
    t, d = x.shape
    nq, _, pc = wqkv.shape
    ng = wg.shape[0]
    tm = TOKEN_TILE
    return pl.pallas_call(
        _proj_kernel,
        grid=(t // tm,),
        in_specs=[
            pl.BlockSpec((tm, d), lambda i: (i, 0)),
            _resident((1, d)),
            _resident((nq, d, pc)),
            _resident((ng, d, pc)),
        ],
        out_specs=[
            pl.BlockSpec((tm, nq * pc), lambda i: (i, 0)),
            pl.BlockSpec((tm, ng * pc), lambda i: (i, 0)),
        ],
        out_shape=[
            jax.ShapeDtypeStruct((t, nq * pc), BF16),
            jax.ShapeDtypeStruct((t, ng * pc), F32),
        ],
        compiler_params=pltpu.CompilerParams(
            dimension_semantics=("arbitrary",), vmem_limit_bytes=VMEM_LIMIT),
        name="proj",
    )(x, gain, wqkv, wg)


def _swa_kernel(sink_ref, q_ref, kprev_ref, k_ref, vprev_ref, v_ref, bias_ref, band_ref,
                o_ref, kwin_ref, vwin_ref):
    j = pl.program_id(1)
    ts = q_ref.shape[0]
    kwin_ref[0:BLOCK, :] = kprev_ref[...]
    kwin_ref[BLOCK:, :] = k_ref[...]
    vwin_ref[0:BLOCK, :] = vprev_ref[...]
    vwin_ref[BLOCK:, :] = v_ref[...]
    lane = lax.broadcasted_iota(jnp.int32, (1, LANES), 1)
    low_half = lane < HEAD_DIM
    key_col = lax.broadcasted_iota(jnp.int32, (1, 2 * BLOCK), 1)
    band = band_ref[...] > 0.0
    group = SWA_Q_HEADS // SWA_KV_HEADS

    def block(n, carry):
        r0 = pl.multiple_of(n * BLOCK, BLOCK)
        first = jnp.logical_and(j == 0, n == 0)
        valid = jnp.logical_and(band, jnp.logical_or(key_col >= BLOCK, jnp.logical_not(first)))
        for p in range(SWA_Q_HEADS // HEADS_PER_TILE):
            g = (p * HEADS_PER_TILE) // group
            q = q_ref[pl.ds(r0, BLOCK), p * LANES:(p + 1) * LANES]
            kw = kwin_ref[pl.ds(r0, 2 * BLOCK), g * LANES:(g + 1) * LANES]
            vw = vwin_ref[pl.ds(r0, 2 * BLOCK), g * LANES:(g + 1) * LANES]
            acc = jnp.zeros((BLOCK, LANES), F32)
            inv = []
            for e in range(HEADS_PER_TILE):
                hd = p * HEADS_PER_TILE + e
                sel = low_half if e == 0 else jnp.logical_not(low_half)
                qe = jnp.where(sel, q, jnp.zeros_like(q))
                ve = jnp.where(sel, vw, jnp.zeros_like(vw))
                logits = _dot_nt(qe, kw)
                logits = jnp.where(valid, logits + bias_ref[hd], NEG_BIG)
                sink = sink_ref[hd]
                m = jnp.maximum(jnp.max(logits, axis=-1, keepdims=True), sink)
                pr = jnp.exp(logits - m)
                den = jnp.sum(pr, axis=-1, keepdims=True) + jnp.exp(sink - m)
                acc = acc + _dot(pr.astype(BF16), ve)
                inv.append(1.0 / den)
            o_ref[pl.ds(r0, BLOCK), p * LANES:(p + 1) * LANES] = (
                acc * jnp.where(low_half, inv[0], inv[1])).astype(o_ref.dtype)
        return carry

    lax.fori_loop(0, ts // BLOCK, block, 0)


def _swa(qkv, sinks, bias, band, *, batch, seq, q_col, k_col, v_col):
    t = qkv.shape[0]
    ts = min(SWA_STEP, seq)
    steps = seq // ts
    bps = ts // BLOCK
    nblk = seq // BLOCK
    qw = SWA_Q_HEADS * HEAD_DIM
    kvw = SWA_KV_HEADS * LANES

    def main(col_blocks):
        return lambda b, j: (b * steps + j, col_blocks)

    def prev(col_blocks):
        return lambda b, j: (b * nblk + jnp.maximum(j * bps - 1, 0), col_blocks)

    return pl.pallas_call(
        _swa_kernel,
        grid=(batch, steps),
        in_specs=[
            pl.BlockSpec(memory_space=pltpu.SMEM),
            pl.BlockSpec((ts, qw), main(q_col // qw)),
            pl.BlockSpec((BLOCK, kvw), prev(k_col // kvw)),
            pl.BlockSpec((ts, kvw), main(k_col // kvw)),
            pl.BlockSpec((BLOCK, kvw), prev(v_col // kvw)),
            pl.BlockSpec((ts, kvw), main(v_col // kvw)),
            _resident(bias.shape),
            _resident(band.shape),
        ],
        out_specs=pl.BlockSpec((ts, qw), lambda b, j: (b * steps + j, 0)),
        out_shape=jax.ShapeDtypeStruct((t, qw), BF16),
        scratch_shapes=[pltpu.VMEM((ts + BLOCK, kvw), BF16), pltpu.VMEM((ts + BLOCK, kvw), BF16)],
        compiler_params=pltpu.CompilerParams(
            dimension_semantics=("arbitrary", "arbitrary"), vmem_limit_bytes=VMEM_LIMIT),
        name="swa",
    )(sinks, qkv, qkv, qkv, qkv, qkv, bias, band)


def _sb_kernel(q_ref, k_ref, v_ref, u_ref, o_ref):
    seq = q_ref.shape[0]
    lane = lax.broadcasted_iota(jnp.int32, (1, LANES), 1)
    low_half = lane < HEAD_DIM
    sels = (low_half, jnp.logical_not(low_half))
    rows = lax.broadcasted_iota(jnp.int32, (SB_TQ, SB_TK), 0)
    cols = lax.broadcasted_iota(jnp.int32, (SB_TQ, SB_TK), 1)
    col_minus_row = cols - rows

    def tile(qs, kb, carries, acc, causal_off):
        ks = pl.multiple_of(kb * SB_TK, SB_TK)
        k = k_ref[pl.ds(ks, SB_TK), :]
        v = v_ref[pl.ds(ks, SB_TK), :]
        u = u_ref[...]
        new_carries = []
        for e in range(HEADS_PER_TILE):
            ve = jnp.where(sels[e], v, jnp.zeros_like(v))
            z = _dot_nt(qs[e], k)
            log_keep = -(jnp.maximum(z, 0.0) + jnp.log1p(jnp.exp(-jnp.abs(z))))
            if causal_off is not None:
                causal = col_minus_row < causal_off
                log_keep = jnp.where(causal, log_keep, 0.0)
            hi = log_keep.astype(BF16)
            lo = (log_keep - hi.astype(F32)).astype(BF16)
            c = _dot(hi, u) + _dot(lo, u) + carries[e]
            a = jnp.exp(z + c)
            if causal_off is not None:
                a = jnp.where(causal, a, 0.0)
            acc = acc + _dot(a.astype(BF16), ve)
            new_carries.append(c[:, 0:1])
        return tuple(new_carries), acc

    def alive(carries):
        return jnp.max(jnp.maximum(carries[0], carries[1])) > SB_LOG_UNDERFLOW

    def qblock(i, carry):
        r0 = pl.multiple_of(i * SB_TQ, SB_TQ)
        q = q_ref[pl.ds(r0, SB_TQ), :]
        qs = tuple(jnp.where(s, q, jnp.zeros_like(q)) for s in sels)
        kb0 = (i * SB_TQ) // SB_TK
        zero = jnp.zeros((SB_TQ, 1), F32)
        carries, acc = tile(qs, kb0, (zero, zero), jnp.zeros((SB_TQ, LANES), F32),
                            i * SB_TQ - kb0 * SB_TK)

        def cond(state):
            return jnp.logical_and(state[0] >= 0, state[1])

        def body(state):
            kb, _, cs, acc_ = state
            cs, acc_ = tile(qs, kb, cs, acc_, None)
            return kb - 1, alive(cs), cs, acc_

        _, _, _, acc = lax.while_loop(cond, body, (kb0 - 1, alive(carries), carries, acc))
        o_ref[pl.ds(r0, SB_TQ), :] = acc.astype(o_ref.dtype)
        return carry

    lax.fori_loop(0, seq // SB_TQ, qblock, 0)


def _sb(qkv, u, *, batch, seq, q_col, k_col, v_col):
    t = qkv.shape[0]
    pairs = SB_HEADS // HEADS_PER_TILE

    def col(c0):
        return lambda b, p: (b, c0 // LANES + p)

    return pl.pallas_call(
        _sb_kernel,
        grid=(batch, pairs),
        in_specs=[
            pl.BlockSpec((seq, LANES), col(q_col)),
            pl.BlockSpec((seq, LANES), col(k_col)),
            pl.BlockSpec((seq, LANES), col(v_col)),
            _resident(u.shape),
        ],
        out_specs=pl.BlockSpec((seq, LANES), lambda b, p: (b, p)),
        out_shape=jax.ShapeDtypeStruct((t, SB_HEADS * HEAD_DIM), BF16),
        compiler_params=pltpu.CompilerParams(
            dimension_semantics=("arbitrary", "arbitrary"), vmem_limit_bytes=VMEM_LIMIT),
        name="sb",
    )(qkv, qkv, qkv, u)


def _merge_kernel(x_ref, oa_ref, ob_ref, g_ref, wa_ref, wb_ref, wo_ref, o_ref):
    d = x_ref.shape[1]
    ma = _dot(oa_ref[...], wa_ref[...])
    mb = _dot(ob_ref[...], wb_ref[...])
    merged = jax.nn.sigmoid(g_ref[:, 0:d]) * ma + jax.nn.sigmoid(g_ref[:, d:2 * d]) * mb
    o_ref[...] = x_ref[...] + _dot(merged.astype(BF16), wo_ref[...])


def _merge(x, oa, ob, gates, wa, wb, wo):
    t, d = x.shape
    tm = TOKEN_TILE
    return pl.pallas_call(
        _merge_kernel,
        grid=(t // tm,),
        in_specs=[
            pl.BlockSpec((tm, d), lambda i: (i, 0)),
            pl.BlockSpec((tm, oa.shape[1]), lambda i: (i, 0)),
            pl.BlockSpec((tm, ob.shape[1]), lambda i: (i, 0)),
            pl.BlockSpec((tm, gates.shape[1]), lambda i: (i, 0)),
            _resident(wa.shape),
            _resident(wb.shape),
            _resident(wo.shape),
        ],
        out_specs=pl.BlockSpec((tm, d), lambda i: (i, 0)),
        out_shape=jax.ShapeDtypeStruct((t, d), F32),
        compiler_params=pltpu.CompilerParams(
            dimension_semantics=("arbitrary",), vmem_limit_bytes=VMEM_LIMIT),
        name="merge",
    )(x, oa, ob, gates, wa, wb, wo)


def _rel_bucket(dist):
    max_exact = REL_BUCKETS // 2
    d = jnp.maximum(dist, 1).astype(F32)
    large = max_exact + (jnp.log(d / max_exact) / math.log(REL_MAX_DIST / max_exact)
                         * (REL_BUCKETS - max_exact)).astype(jnp.int32)
    large = jnp.minimum(large, REL_BUCKETS - 1)
    return jnp.where(dist < max_exact, dist, large)


def _swa_bias(rel_table):
    qi = jnp.arange(BLOCK)[:, None] + BLOCK
    kj = jnp.arange(2 * BLOCK)[None, :]
    dist = qi - kj
    band = ((dist >= 0) & (dist < SWA_WINDOW)).astype(F32)
    bias = rel_table.astype(F32)[_rel_bucket(jnp.maximum(dist, 0))]
    return bias.transpose(2, 0, 1), band


def _col_chunks(w, chunk):
    d_in, n = w.shape
    return w.reshape(d_in, n // chunk, chunk).transpose(1, 0, 2).astype(BF16)


def _dup_heads(w):
    d_in = w.shape[0]
    w = w.reshape(d_in, SWA_KV_HEADS, 1, HEAD_DIM)
    return jnp.broadcast_to(w, (d_in, SWA_KV_HEADS, HEADS_PER_TILE, HEAD_DIM)).reshape(d_in, -1)


def kernel(x, norm_ffn1, ffn1_w1, ffn1_w3, ffn1_w2, norm_mix, w_in, swa_sinks, rel_bias,
           w_branch_swa, w_branch_sb, w_out, norm_ffn2, ffn2_w1, ffn2_w3, ffn2_w2, norm_final):
    batch, seq, d = x.shape
    depth = norm_ffn1.shape[0]
    d_ff = ffn1_w1.shape[-1]
    qa_w = SWA_Q_HEADS * HEAD_DIM
    kva_w = SWA_KV_HEADS * HEAD_DIM
    sb_w = SB_HEADS * HEAD_DIM
    scale = HEAD_DIM ** -0.5

    bias, band = _swa_bias(rel_bias)
    u = (jnp.arange(SB_TK)[:, None] >= jnp.arange(SB_TK)[None, :]).astype(BF16)
    gain_final = norm_final.reshape(1, d)

    xt = x.reshape(batch * seq, d)
    for layer in range(depth):
        last = layer == depth - 1
        xt = _ffn(xt, norm_ffn1[layer].reshape(1, d),
                  _col_chunks(ffn1_w1[layer], FFN_CHUNK), _col_chunks(ffn1_w3[layer], FFN_CHUNK),
                  ffn1_w2[layer].reshape(d_ff // FFN_CHUNK, FFN_CHUNK, d).astype(BF16),
                  gain_final, final_norm=False)

        w = w_in[layer]
        o = 0
        cols = {}
        for name, width in (("qa", qa_w), ("ka", kva_w), ("va", kva_w), ("qb", sb_w), ("kb", sb_w),
                            ("vb", sb_w), ("ga", d), ("gb", d)):
            cols[name] = w[:, o:o + width]
            o += width
        wqkv = jnp.concatenate([cols["qa"] * scale, _dup_heads(cols["ka"]), _dup_heads(cols["va"]),
                                cols["qb"] * scale, cols["kb"], cols["vb"]], axis=1)
        wg = jnp.concatenate([cols["ga"], cols["gb"]], axis=1)
        qkv, gates = _proj(xt, norm_mix[layer].reshape(1, d),
                           _col_chunks(wqkv, PROJ_CHUNK), _col_chunks(wg, PROJ_CHUNK))

        kva_dup = SWA_KV_HEADS * LANES
        c_qa, c_ka, c_va = 0, qa_w, qa_w + kva_dup
        c_qb = qa_w + 2 * kva_dup
        c_kb, c_vb = c_qb + sb_w, c_qb + 2 * sb_w
        oa = _swa(qkv, swa_sinks[layer], bias, band, batch=batch, seq=seq,
                  q_col=c_qa, k_col=c_ka, v_col=c_va)
        ob = _sb(qkv, u, batch=batch, seq=seq, q_col=c_qb, k_col=c_kb, v_col=c_vb)

        xt = _merge(xt, oa, ob, gates, w_branch_swa[layer].astype(BF16),
                    w_branch_sb[layer].astype(BF16), w_out[layer].astype(BF16))

        xt = _ffn(xt, norm_ffn2[layer].reshape(1, d),
                  _col_chunks(ffn2_w1[layer], FFN_CHUNK), _col_chunks(ffn2_w3[layer], FFN_CHUNK),
                  ffn2_w2[layer].reshape(d_ff // FFN_CHUNK, FFN_CHUNK, d).astype(BF16),
                  gain_final, final_norm=last)
    return xt.reshape(batch, seq, d)
```

```python
import functools
import math

import jax
import jax.numpy as jnp
import numpy as np
from jax import lax
from jax.experimental import pallas as pl
from jax.experimental.pallas import tpu as pltpu

F32 = jnp.float32
BF16 = jnp.bfloat16

HEAD_DIM = 64
SWA_Q_HEADS = 8
SWA_KV_HEADS = 2
SWA_WINDOW = 128
SB_HEADS = 8
BLOCK = 128
REL_BUCKETS = 32
REL_MAX_DIST = 128
RMS_EPS = 1e-6
NEG_BIG = -1e30

LANES = 128
HEADS_PER_TILE = LANES // HEAD_DIM
FFN_CHUNK = 256
PROJ_CHUNK = 256
TOKEN_TILE = 512
SB_TQ = 256
SB_TK = 256
LOG2E = math.log2(math.e)
SB_LOG2_UNDERFLOW = -150.0
SWA_STEP = 1024
VMEM_LIMIT = 60000 * 1024


def _rms(x, g):
    ms = jnp.mean(x * x, axis=-1, keepdims=True)
    return x * lax.rsqrt(ms + RMS_EPS) * g


def _dot(a, b):
    return jnp.dot(a, b, preferred_element_type=F32)


def _dot_nt(a, b):
    return lax.dot_general(a, b, (((1,), (1,)), ((), ())), preferred_element_type=F32)


def _resident(shape):
    nd = len(shape)
    return pl.BlockSpec(shape, lambda *_: (0,) * nd, pipeline_mode=pl.Buffered(1))


def _ffn_kernel(x_ref, g_ref, w1_ref, w3_ref, w2_ref, gf_ref, o_ref, acc_ref, *, final_norm):
    x = x_ref[...]
    h = _rms(x, g_ref[...]).astype(BF16)
    acc_ref[...] = jnp.zeros_like(acc_ref)

    def chunk(c, carry):
        a = _dot(h, w1_ref[c])
        b = _dot(h, w3_ref[c])
        gated = (a * jax.nn.sigmoid(a) * b).astype(BF16)
        acc_ref[...] += _dot(gated, w2_ref[c])
        return carry

    lax.fori_loop(0, w1_ref.shape[0], chunk, 0)
    y = x + 0.5 * acc_ref[...]
    if final_norm:
        y = _rms(y, gf_ref[...])
    o_ref[...] = y


def _ffn(x, gain, w1c, w3c, w2c, gain_final, *, final_norm):
    t, d = x.shape
    nch, _, fc = w1c.shape
    tm = TOKEN_TILE
    return pl.pallas_call(
        functools.partial(_ffn_kernel, final_norm=final_norm),
        grid=(t // tm,),
        in_specs=[
            pl.BlockSpec((tm, d), lambda i: (i, 0)),
            _resident((1, d)),
            _resident((nch, d, fc)),
            _resident((nch, d, fc)),
            _resident((nch, fc, d)),
            _resident((1, d)),
        ],
        out_specs=pl.BlockSpec((tm, d), lambda i: (i, 0)),
        out_shape=jax.ShapeDtypeStruct((t, d), F32),
        scratch_shapes=[pltpu.VMEM((tm, d), F32)],
        compiler_params=pltpu.CompilerParams(
            dimension_semantics=("arbitrary",), vmem_limit_bytes=VMEM_LIMIT),
        name="ffn_final" if final_norm else "ffn",
    )(x, gain, w1c, w3c, w2c, gain_final)


def _proj_kernel(x_ref, g_ref, wqkv_ref, wg_ref, qkv_ref, gate_ref):
    h = _rms(x_ref[...], g_ref[...]).astype(BF16)
    for c in range(wqkv_ref.shape[0]):
        qkv_ref[:, c * PROJ_CHUNK:(c + 1) * PROJ_CHUNK] = _dot(h, wqkv_ref[c]).astype(BF16)
    for c in range(wg_ref.shape[0]):
        gate_ref[:, c * PROJ_CHUNK:(c + 1) * PROJ_CHUNK] = _dot(h, wg_ref[c])


def _proj(x, gain, wqkv, wg):
    t, d = x.shape
    nq, _, pc = wqkv.shape
    ng = wg.shape[0]
    tm = TOKEN_TILE
    return pl.pallas_call(
        _proj_kernel,
        grid=(t // tm,),
        in_specs=[
            pl.BlockSpec((tm, d), lambda i: (i, 0)),
            _resident((1, d)),
            _resident((nq, d, pc)),
            _resident((ng, d, pc)),
        ],
        out_specs=[
            pl.BlockSpec((tm, nq * pc), lambda i: (i, 0)),
            pl.BlockSpec((tm, ng * pc), lambda i: (i, 0)),
        ],
        out_shape=[
            jax.ShapeDtypeStruct((t, nq * pc), BF16),
            jax.ShapeDtypeStruct((t, ng * pc), F32),
        ],
        compiler_params=pltpu.CompilerParams(
            dimension_semantics=("arbitrary",), vmem_limit_bytes=VMEM_LIMIT),
        name="proj",
    )(x, gain, wqkv, wg)


def _swa_kernel(sink_ref, q_ref, kprev_ref, k_ref, vprev_ref, v_ref, bias_ref, band_ref,
                o_ref, kwin_ref, vwin_ref):
    j = pl.program_id(1)
    ts = q_ref.shape[0]
    kwin_ref[0:BLOCK, :] = kprev_ref[...]
    kwin_ref[BLOCK:, :] = k_ref[...]
    vwin_ref[0:BLOCK, :] = vprev_ref[...]
    vwin_ref[BLOCK:, :] = v_ref[...]
    lane = lax.broadcasted_iota(jnp.int32, (1, LANES), 1)
    low_half = lane < HEAD_DIM
    key_col = lax.broadcasted_iota(jnp.int32, (1, 2 * BLOCK), 1)
    band = band_ref[...] > 0.0
    group = SWA_Q_HEADS // SWA_KV_HEADS

    def block(n, carry):
        r0 = pl.multiple_of(n * BLOCK, BLOCK)
        first = jnp.logical_and(j == 0, n == 0)
        valid = jnp.logical_and(band, jnp.logical_or(key_col >= BLOCK, jnp.logical_not(first)))
        for p in range(SWA_Q_HEADS // HEADS_PER_TILE):
            g = (p * HEADS_PER_TILE) // group
            q = q_ref[pl.ds(r0, BLOCK), p * LANES:(p + 1) * LANES]
            kw = kwin_ref[pl.ds(r0, 2 * BLOCK), g * LANES:(g + 1) * LANES]
            vw = vwin_ref[pl.ds(r0, 2 * BLOCK), g * LANES:(g + 1) * LANES]
            acc = jnp.zeros((BLOCK, LANES), F32)
            inv = []
            for e in range(HEADS_PER_TILE):
                hd = p * HEADS_PER_TILE + e
                sel = low_half if e == 0 else jnp.logical_not(low_half)
                qe = jnp.where(sel, q, jnp.zeros_like(q))
                ve = jnp.where(sel, vw, jnp.zeros_like(vw))
                logits = _dot_nt(qe, kw)
                logits = jnp.where(valid, logits + bias_ref[hd], NEG_BIG)
                sink = sink_ref[hd]
                m = jnp.maximum(jnp.max(logits, axis=-1, keepdims=True), sink)
                pr = jnp.exp(logits - m)
                den = jnp.sum(pr, axis=-1, keepdims=True) + jnp.exp(sink - m)
                acc = acc + _dot(pr.astype(BF16), ve)
                inv.append(1.0 / den)
            o_ref[pl.ds(r0, BLOCK), p * LANES:(p + 1) * LANES] = (
                acc * jnp.where(low_half, inv[0], inv[1])).astype(o_ref.dtype)
        return carry

    lax.fori_loop(0, ts // BLOCK, block, 0)


def _swa(qkv, sinks, bias, band, *, batch, seq, q_col, k_col, v_col):
    t = qkv.shape[0]
    ts = min(SWA_STEP, seq)
    steps = seq // ts
    bps = ts // BLOCK
    nblk = seq // BLOCK
    qw = SWA_Q_HEADS * HEAD_DIM
    kvw = SWA_KV_HEADS * LANES

    def main(col_blocks):
        return lambda b, j: (b * steps + j, col_blocks)

    def prev(col_blocks):
        return lambda b, j: (b * nblk + jnp.maximum(j * bps - 1, 0), col_blocks)

    return pl.pallas_call(
        _swa_kernel,
        grid=(batch, steps),
        in_specs=[
            pl.BlockSpec(memory_space=pltpu.SMEM),
            pl.BlockSpec((ts, qw), main(q_col // qw)),
            pl.BlockSpec((BLOCK, kvw), prev(k_col // kvw)),
            pl.BlockSpec((ts, kvw), main(k_col // kvw)),
            pl.BlockSpec((BLOCK, kvw), prev(v_col // kvw)),
            pl.BlockSpec((ts, kvw), main(v_col // kvw)),
            _resident(bias.shape),
            _resident(band.shape),
        ],
        out_specs=pl.BlockSpec((ts, qw), lambda b, j: (b * steps + j, 0)),
        out_shape=jax.ShapeDtypeStruct((t, qw), BF16),
        scratch_shapes=[pltpu.VMEM((ts + BLOCK, kvw), BF16), pltpu.VMEM((ts + BLOCK, kvw), BF16)],
        compiler_params=pltpu.CompilerParams(
            dimension_semantics=("arbitrary", "arbitrary"), vmem_limit_bytes=VMEM_LIMIT),
        name="swa",
    )(sinks, qkv, qkv, qkv, qkv, qkv, bias, band)


def _sb_kernel(q_ref, k_ref, v_ref, u_ref, o_ref):
    seq = q_ref.shape[0]
    lane = lax.broadcasted_iota(jnp.int32, (1, LANES), 1)
    low_half = lane < HEAD_DIM
    sels = (low_half, jnp.logical_not(low_half))
    rows = lax.broadcasted_iota(jnp.int32, (SB_TQ, SB_TK), 0)
    cols = lax.broadcasted_iota(jnp.int32, (SB_TQ, SB_TK), 1)
    col_minus_row = cols - rows

    def tile(qs, kb, carries, acc, causal_off):
        ks = pl.multiple_of(kb * SB_TK, SB_TK)
        k = k_ref[pl.ds(ks, SB_TK), :]
        v = v_ref[pl.ds(ks, SB_TK), :]
        u = u_ref[...]
        new_carries = []
        for e in range(HEADS_PER_TILE):
            ve = jnp.where(sels[e], v, jnp.zeros_like(v))
            w2 = _dot_nt(qs[e], k) * LOG2E
            neg_abs = lax.bitcast_convert_type(
                lax.bitcast_convert_type(w2, jnp.uint32) | jnp.uint32(0x80000000), F32)
            log_keep = jnp.minimum(w2, 0.0) - jnp.log2(1.0 + jnp.exp2(neg_abs))
            if causal_off is not None:
                causal = col_minus_row < causal_off
                log_keep = jnp.where(causal, log_keep, 0.0)
            hi = log_keep.astype(BF16)
            lo = (log_keep - hi.astype(F32)).astype(BF16)
            c = _dot(hi, u) + _dot(lo, u) + carries[e]
            a = jnp.exp2(c - w2)
            if causal_off is not None:
                a = jnp.where(causal, a, 0.0)
            acc = acc + _dot(a.astype(BF16), ve)
            new_carries.append(c[:, 0:1])
        return tuple(new_carries), acc

    def alive(carries):
        return jnp.max(jnp.maximum(carries[0], carries[1])) > SB_LOG2_UNDERFLOW

    def qblock(i, static_prev):
        r0 = pl.multiple_of(i * SB_TQ, SB_TQ)
        q = q_ref[pl.ds(r0, SB_TQ), :]
        qs = tuple(jnp.where(s, q, jnp.zeros_like(q)) for s in sels)
        zero = jnp.zeros((SB_TQ, 1), F32)
        carries, acc = tile(qs, i, (zero, zero), jnp.zeros((SB_TQ, LANES), F32), 0)
        if static_prev:
            carries, acc = tile(qs, i - 1, carries, acc, None)

            def cond(state):
                return jnp.logical_and(state[0] >= 0, state[1])

            def body(state):
                kb, _, cs, acc_ = state
                cs, acc_ = tile(qs, kb, cs, acc_, None)
                return kb - 1, alive(cs), cs, acc_

            _, _, _, acc = lax.while_loop(cond, body, (i - 2, alive(carries), carries, acc))
        o_ref[pl.ds(r0, SB_TQ), :] = acc.astype(o_ref.dtype)

    qblock(0, False)

    def step(i, carry):
        qblock(i, True)
        return carry

    lax.fori_loop(1, seq // SB_TQ, step, 0)


def _sb(qkv, u, *, batch, seq, q_col, k_col, v_col):
    t = qkv.shape[0]
    pairs = SB_HEADS // HEADS_PER_TILE

    def col(c0):
        return lambda b, p: (b, c0 // LANES + p)

    return pl.pallas_call(
        _sb_kernel,
        grid=(batch, pairs),
        in_specs=[
            pl.BlockSpec((seq, LANES), col(q_col)),
            pl.BlockSpec((seq, LANES), col(k_col)),
            pl.BlockSpec((seq, LANES), col(v_col)),
            _resident(u.shape),
        ],
        out_specs=pl.BlockSpec((seq, LANES), lambda b, p: (b, p)),
        out_shape=jax.ShapeDtypeStruct((t, SB_HEADS * HEAD_DIM), BF16),
        compiler_params=pltpu.CompilerParams(
            dimension_semantics=("arbitrary", "arbitrary"), vmem_limit_bytes=VMEM_LIMIT),
        name="sb",
    )(qkv, qkv, qkv, u)


def _merge_kernel(x_ref, oa_ref, ob_ref, g_ref, wa_ref, wb_ref, wo_ref, o_ref):
    d = x_ref.shape[1]
    ma = _dot(oa_ref[...], wa_ref[...])
    mb = _dot(ob_ref[...], wb_ref[...])
    merged = jax.nn.sigmoid(g_ref[:, 0:d]) * ma + jax.nn.sigmoid(g_ref[:, d:2 * d]) * mb
    o_ref[...] = x_ref[...] + _dot(merged.astype(BF16), wo_ref[...])


def _merge(x, oa, ob, gates, wa, wb, wo):
    t, d = x.shape
    tm = TOKEN_TILE
    return pl.pallas_call(
        _merge_kernel,
        grid=(t // tm,),
        in_specs=[
            pl.BlockSpec((tm, d), lambda i: (i, 0)),
            pl.BlockSpec((tm, oa.shape[1]), lambda i: (i, 0)),
            pl.BlockSpec((tm, ob.shape[1]), lambda i: (i, 0)),
            pl.BlockSpec((tm, gates.shape[1]), lambda i: (i, 0)),
            _resident(wa.shape),
            _resident(wb.shape),
            _resident(wo.shape),
        ],
        out_specs=pl.BlockSpec((tm, d), lambda i: (i, 0)),
        out_shape=jax.ShapeDtypeStruct((t, d), F32),
        compiler_params=pltpu.CompilerParams(
            dimension_semantics=("arbitrary",), vmem_limit_bytes=VMEM_LIMIT),
        name="merge",
    )(x, oa, ob, gates, wa, wb, wo)


def _rel_bucket(dist):
    max_exact = REL_BUCKETS // 2
    d = jnp.maximum(dist, 1).astype(F32)
    large = max_exact + (jnp.log(d / max_exact) / math.log(REL_MAX_DIST / max_exact)
                         * (REL_BUCKETS - max_exact)).astype(jnp.int32)
    large = jnp.minimum(large, REL_BUCKETS - 1)
    return jnp.where(dist < max_exact, dist, large)


def _swa_bias(rel_table):
    qi = jnp.arange(BLOCK)[:, None] + BLOCK
    kj = jnp.arange(2 * BLOCK)[None, :]
    dist = qi - kj
    band = ((dist >= 0) & (dist < SWA_WINDOW)).astype(F32)
    bias = rel_table.astype(F32)[_rel_bucket(jnp.maximum(dist, 0))]
    return bias.transpose(2, 0, 1), band


def _col_chunks(w, chunk):
    d_in, n = w.shape
    return w.reshape(d_in, n // chunk, chunk).transpose(1, 0, 2).astype(BF16)


def _dup_heads(w):
    d_in = w.shape[0]
    w = w.reshape(d_in, SWA_KV_HEADS, 1, HEAD_DIM)
    return jnp.broadcast_to(w, (d_in, SWA_KV_HEADS, HEADS_PER_TILE, HEAD_DIM)).reshape(d_in, -1)


def kernel(x, norm_ffn1, ffn1_w1, ffn1_w3, ffn1_w2, norm_mix, w_in, swa_sinks, rel_bias,
           w_branch_swa, w_branch_sb, w_out, norm_ffn2, ffn2_w1, ffn2_w3, ffn2_w2, norm_final):
    batch, seq, d = x.shape
    depth = norm_ffn1.shape[0]
    d_ff = ffn1_w1.shape[-1]
    qa_w = SWA_Q_HEADS * HEAD_DIM
    kva_w = SWA_KV_HEADS * HEAD_DIM
    sb_w = SB_HEADS * HEAD_DIM
    scale = HEAD_DIM ** -0.5

    bias, band = _swa_bias(rel_bias)
    u = (jnp.arange(SB_TK)[:, None] >= jnp.arange(SB_TK)[None, :]).astype(BF16)
    gain_final = norm_final.reshape(1, d)

    xt = x.reshape(batch * seq, d)
    for layer in range(depth):
        last = layer == depth - 1
        xt = _ffn(xt, norm_ffn1[layer].reshape(1, d),
                  _col_chunks(ffn1_w1[layer], FFN_CHUNK), _col_chunks(ffn1_w3[layer], FFN_CHUNK),
                  ffn1_w2[layer].reshape(d_ff // FFN_CHUNK, FFN_CHUNK, d).astype(BF16),
                  gain_final, final_norm=False)

        w = w_in[layer]
        o = 0
        cols = {}
        for name, width in (("qa", qa_w), ("ka", kva_w), ("va", kva_w), ("qb", sb_w), ("kb", sb_w),
                            ("vb", sb_w), ("ga", d), ("gb", d)):
            cols[name] = w[:, o:o + width]
            o += width
        wqkv = jnp.concatenate([cols["qa"] * scale, _dup_heads(cols["ka"]), _dup_heads(cols["va"]),
                                cols["qb"] * -scale, cols["kb"], cols["vb"]], axis=1)
        wg = jnp.concatenate([cols["ga"], cols["gb"]], axis=1)
        qkv, gates = _proj(xt, norm_mix[layer].reshape(1, d),
                           _col_chunks(wqkv, PROJ_CHUNK), _col_chunks(wg, PROJ_CHUNK))

        kva_dup = SWA_KV_HEADS * LANES
        c_qa, c_ka, c_va = 0, qa_w, qa_w + kva_dup
        c_qb = qa_w + 2 * kva_dup
        c_kb, c_vb = c_qb + sb_w, c_qb + 2 * sb_w
        oa = _swa(qkv, swa_sinks[layer], bias, band, batch=batch, seq=seq,
                  q_col=c_qa, k_col=c_ka, v_col=c_va)
        ob = _sb(qkv, u, batch=batch, seq=seq, q_col=c_qb, k_col=c_kb, v_col=c_vb)

        xt = _merge(xt, oa, ob, gates, w_branch_swa[layer].astype(BF16),
                    w_branch_sb[layer].astype(BF16), w_out[layer].astype(BF16))

        xt = _ffn(xt, norm_ffn2[layer].reshape(1, d),
                  _col_chunks(ffn2_w1[layer], FFN_CHUNK), _col_chunks(ffn2_w3[layer], FFN_CHUNK),
                  ffn2_w2[layer].reshape(d_ff // FFN_CHUNK, FFN_CHUNK, d).astype(BF16),
                  gain_final, final_norm=last)
    return xt.reshape(batch, seq, d)
```

```python
import functools
import math

import jax
import jax.numpy as jnp
import numpy as np
from jax import lax
from jax.experimental import pallas as pl
from jax.experimental.pallas import tpu as pltpu

F32 = jnp.float32
BF16 = jnp.bfloat16

HEAD_DIM = 64
SWA_Q_HEADS = 8
SWA_KV_HEADS = 2
SWA_WINDOW = 128
SB_HEADS = 8
BLOCK = 128
REL_BUCKETS = 32
REL_MAX_DIST = 128
RMS_EPS = 1e-6
NEG_BIG = -1e30

LANES = 128
HEADS_PER_TILE = LANES // HEAD_DIM
FFN_CHUNK = 256
PROJ_CHUNK = 256
TOKEN_TILE = 512
SB_TQ = 256
SB_TK = 256
LOG2E = math.log2(math.e)
SB_LOG2_UNDERFLOW = -150.0
SWA_STEP = 1024
VMEM_LIMIT = 60000 * 1024


def _rms(x, g):
    ms = jnp.mean(x * x, axis=-1, keepdims=True)
    return x * lax.rsqrt(ms + RMS_EPS) * g


def _dot(a, b):
    return jnp.dot(a, b, preferred_element_type=F32)


def _dot_nt(a, b):
    return lax.dot_general(a, b, (((1,), (1,)), ((), ())), preferred_element_type=F32)


def _resident(shape):
    nd = len(shape)
    return pl.BlockSpec(shape, lambda *_: (0,) * nd, pipeline_mode=pl.Buffered(1))


def _swiglu_into(acc_ref, h, w1_ref, w3_ref, w2_ref):
    d_ff = w1_ref.shape[1]
    for c in range(d_ff // FFN_CHUNK):
        cols = slice(c * FFN_CHUNK, (c + 1) * FFN_CHUNK)
        a = _dot(h, w1_ref[:, cols])
        b = _dot(h, w3_ref[:, cols])
        gated = (a * jax.nn.sigmoid(a) * b).astype(BF16)
        down = _dot(gated, w2_ref[cols, :])
        if c == 0:
            acc_ref[...] = down
        else:
            acc_ref[...] += down


def _pre_kernel(x_ref, g1_ref, w1_ref, w3_ref, w2_ref, gm_ref, wqkv_ref, wg_ref,
                x1_ref, qkv_ref, gate_ref, acc_ref):
    x = x_ref[...]
    _swiglu_into(acc_ref, _rms(x, g1_ref[...]).astype(BF16), w1_ref, w3_ref, w2_ref)
    x1 = x + 0.5 * acc_ref[...]
    x1_ref[...] = x1
    h = _rms(x1, gm_ref[...]).astype(BF16)
    for c in range(qkv_ref.shape[1] // PROJ_CHUNK):
        cols = slice(c * PROJ_CHUNK, (c + 1) * PROJ_CHUNK)
        qkv_ref[:, cols] = _dot(h, wqkv_ref[:, cols]).astype(BF16)
    for c in range(gate_ref.shape[1] // PROJ_CHUNK):
        cols = slice(c * PROJ_CHUNK, (c + 1) * PROJ_CHUNK)
        gate_ref[:, cols] = jax.nn.sigmoid(_dot(h, wg_ref[:, cols])).astype(BF16)


def _pre(x, g1, w1, w3, w2, gm, wqkv, wg):
    t, d = x.shape
    tm = TOKEN_TILE
    row = lambda i: (i, 0)
    return pl.pallas_call(
        _pre_kernel,
        grid=(t // tm,),
        in_specs=[pl.BlockSpec((tm, d), row), _resident(g1.shape), _resident(w1.shape),
                  _resident(w3.shape), _resident(w2.shape), _resident(gm.shape),
                  _resident(wqkv.shape), _resident(wg.shape)],
        out_specs=[pl.BlockSpec((tm, d), row), pl.BlockSpec((tm, wqkv.shape[1]), row),
                   pl.BlockSpec((tm, wg.shape[1]), row)],
        out_shape=[jax.ShapeDtypeStruct((t, d), F32),
                   jax.ShapeDtypeStruct((t, wqkv.shape[1]), BF16),
                   jax.ShapeDtypeStruct((t, wg.shape[1]), BF16)],
        scratch_shapes=[pltpu.VMEM((tm, d), F32)],
        compiler_params=pltpu.CompilerParams(
            dimension_semantics=("arbitrary",), vmem_limit_bytes=VMEM_LIMIT),
        name="pre",
    )(x, g1, w1, w3, w2, gm, wqkv, wg)


def _post_kernel(x_ref, oa_ref, ob_ref, gate_ref, wa_ref, wb_ref, wo_ref, g2_ref,
                 w1_ref, w3_ref, w2_ref, gf_ref, o_ref, acc_ref, mrg_ref, *, final_norm):
    d = x_ref.shape[1]
    oa = oa_ref[...]
    ob = ob_ref[...]
    for c in range(d // PROJ_CHUNK):
        cols = slice(c * PROJ_CHUNK, (c + 1) * PROJ_CHUNK)
        gcols = slice(d + c * PROJ_CHUNK, d + (c + 1) * PROJ_CHUNK)
        merged = (gate_ref[:, cols].astype(F32) * _dot(oa, wa_ref[:, cols])
                  + gate_ref[:, gcols].astype(F32) * _dot(ob, wb_ref[:, cols]))
        mrg_ref[:, cols] = merged.astype(BF16)
    x2 = x_ref[...] + _dot(mrg_ref[...], wo_ref[...])
    _swiglu_into(acc_ref, _rms(x2, g2_ref[...]).astype(BF16), w1_ref, w3_ref, w2_ref)
    y = x2 + 0.5 * acc_ref[...]
    if final_norm:
        y = _rms(y, gf_ref[...])
    o_ref[...] = y


def _post(x, oa, ob, gate, wa, wb, wo, g2, w1, w3, w2, gf, *, final_norm):
    t, d = x.shape
    tm = TOKEN_TILE
    row = lambda i: (i, 0)
    return pl.pallas_call(
        functools.partial(_post_kernel, final_norm=final_norm),
        grid=(t // tm,),
        in_specs=[pl.BlockSpec((tm, d), row), pl.BlockSpec((tm, oa.shape[1]), row),
                  pl.BlockSpec((tm, ob.shape[1]), row), pl.BlockSpec((tm, gate.shape[1]), row),
                  _resident(wa.shape), _resident(wb.shape), _resident(wo.shape),
                  _resident(g2.shape), _resident(w1.shape), _resident(w3.shape),
                  _resident(w2.shape), _resident(gf.shape)],
        out_specs=pl.BlockSpec((tm, d), row),
        out_shape=jax.ShapeDtypeStruct((t, d), F32),
        scratch_shapes=[pltpu.VMEM((tm, d), F32), pltpu.VMEM((tm, d), BF16)],
        compiler_params=pltpu.CompilerParams(
            dimension_semantics=("arbitrary",), vmem_limit_bytes=VMEM_LIMIT),
        name="post",
    )(x, oa, ob, gate, wa, wb, wo, g2, w1, w3, w2, gf)


def _swa_kernel(sink_ref, q_ref, kprev_ref, k_ref, vprev_ref, v_ref, bias_ref, band_ref,
                o_ref, kwin_ref, vwin_ref):
    j = pl.program_id(1)
    ts = q_ref.shape[0]
    kwin_ref[0:BLOCK, :] = kprev_ref[...]
    kwin_ref[BLOCK:, :] = k_ref[...]
    vwin_ref[0:BLOCK, :] = vprev_ref[...]
    vwin_ref[BLOCK:, :] = v_ref[...]
    lane = lax.broadcasted_iota(jnp.int32, (1, LANES), 1)
    low_half = lane < HEAD_DIM
    key_col = lax.broadcasted_iota(jnp.int32, (1, 2 * BLOCK), 1)
    band = band_ref[...] > 0.0
    group = SWA_Q_HEADS // SWA_KV_HEADS

    def block(n, carry):
        r0 = pl.multiple_of(n * BLOCK, BLOCK)
        first = jnp.logical_and(j == 0, n == 0)
        valid = jnp.logical_and(band, jnp.logical_or(key_col >= BLOCK, jnp.logical_not(first)))
        for p in range(SWA_Q_HEADS // HEADS_PER_TILE):
            g = (p * HEADS_PER_TILE) // group
            q = q_ref[pl.ds(r0, BLOCK), p * LANES:(p + 1) * LANES]
            kw = kwin_ref[pl.ds(r0, 2 * BLOCK), g * LANES:(g + 1) * LANES]
            vw = vwin_ref[pl.ds(r0, 2 * BLOCK), g * LANES:(g + 1) * LANES]
            acc = jnp.zeros((BLOCK, LANES), F32)
            inv = []
            for e in range(HEADS_PER_TILE):
                hd = p * HEADS_PER_TILE + e
                sel = low_half if e == 0 else jnp.logical_not(low_half)
                qe = jnp.where(sel, q, jnp.zeros_like(q))
                ve = jnp.where(sel, vw, jnp.zeros_like(vw))
                logits = _dot_nt(qe, kw)
                logits = jnp.where(valid, logits + bias_ref[hd], NEG_BIG)
                sink = sink_ref[hd]
                m = jnp.maximum(jnp.max(logits, axis=-1, keepdims=True), sink)
                pr = jnp.exp(logits - m)
                den = jnp.sum(pr, axis=-1, keepdims=True) + jnp.exp(sink - m)
                acc = acc + _dot(pr.astype(BF16), ve)
                inv.append(1.0 / den)
            o_ref[pl.ds(r0, BLOCK), p * LANES:(p + 1) * LANES] = (
                acc * jnp.where(low_half, inv[0], inv[1])).astype(o_ref.dtype)
        return carry

    lax.fori_loop(0, ts // BLOCK, block, 0, unroll=2)


def _swa(qkv, sinks, bias, band, *, batch, seq, q_col, k_col, v_col):
    t = qkv.shape[0]
    ts = min(SWA_STEP, seq)
    steps = seq // ts
    bps = ts // BLOCK
    nblk = seq // BLOCK
    qw = SWA_Q_HEADS * HEAD_DIM
    kvw = SWA_KV_HEADS * LANES

    def main(col_blocks):
        return lambda b, j: (b * steps + j, col_blocks)

    def prev(col_blocks):
        return lambda b, j: (b * nblk + jnp.maximum(j * bps - 1, 0), col_blocks)

    return pl.pallas_call(
        _swa_kernel,
        grid=(batch, steps),
        in_specs=[
            pl.BlockSpec(memory_space=pltpu.SMEM),
            pl.BlockSpec((ts, qw), main(q_col // qw)),
            pl.BlockSpec((BLOCK, kvw), prev(k_col // kvw)),
            pl.BlockSpec((ts, kvw), main(k_col // kvw)),
            pl.BlockSpec((BLOCK, kvw), prev(v_col // kvw)),
            pl.BlockSpec((ts, kvw), main(v_col // kvw)),
            _resident(bias.shape),
            _resident(band.shape),
        ],
        out_specs=pl.BlockSpec((ts, qw), lambda b, j: (b * steps + j, 0)),
        out_shape=jax.ShapeDtypeStruct((t, qw), BF16),
        scratch_shapes=[pltpu.VMEM((ts + BLOCK, kvw), BF16), pltpu.VMEM((ts + BLOCK, kvw), BF16)],
        compiler_params=pltpu.CompilerParams(
            dimension_semantics=("arbitrary", "arbitrary"), vmem_limit_bytes=VMEM_LIMIT),
        name="swa",
    )(sinks, qkv, qkv, qkv, qkv, qkv, bias, band)


def _sb_kernel(q_ref, k_ref, v_ref, u_ref, o_ref):
    seq = q_ref.shape[0]
    lane = lax.broadcasted_iota(jnp.int32, (1, LANES), 1)
    low_half = lane < HEAD_DIM
    sels = (low_half, jnp.logical_not(low_half))
    rows = lax.broadcasted_iota(jnp.int32, (SB_TQ, SB_TK), 0)
    cols = lax.broadcasted_iota(jnp.int32, (SB_TQ, SB_TK), 1)
    col_minus_row = cols - rows

    def tile(qs, kb, carries, acc, causal_off):
        ks = pl.multiple_of(kb * SB_TK, SB_TK)
        k = k_ref[pl.ds(ks, SB_TK), :]
        v = v_ref[pl.ds(ks, SB_TK), :]
        u = u_ref[...]
        new_carries = []
        for e in range(HEADS_PER_TILE):
            ve = jnp.where(sels[e], v, jnp.zeros_like(v))
            w2 = _dot_nt(qs[e], k) * LOG2E
            neg_abs = lax.bitcast_convert_type(
                lax.bitcast_convert_type(w2, jnp.uint32) | jnp.uint32(0x80000000), F32)
            log_keep = jnp.minimum(w2, 0.0) - jnp.log2(1.0 + jnp.exp2(neg_abs))
            if causal_off is not None:
                causal = col_minus_row < causal_off
                log_keep = jnp.where(causal, log_keep, 0.0)
            hi = log_keep.astype(BF16)
            lo = (log_keep - hi.astype(F32)).astype(BF16)
            c = _dot(hi, u) + _dot(lo, u) + carries[e]
            a = jnp.exp2(c - w2)
            if causal_off is not None:
                a = jnp.where(causal, a, 0.0)
            acc = acc + _dot(a.astype(BF16), ve)
            new_carries.append(c[:, 0:1])
        return tuple(new_carries), acc

    def alive(carries):
        return jnp.max(jnp.maximum(carries[0], carries[1])) > SB_LOG2_UNDERFLOW

    def qblock(i, static_prev):
        r0 = pl.multiple_of(i * SB_TQ, SB_TQ)
        q = q_ref[pl.ds(r0, SB_TQ), :]
        qs = tuple(jnp.where(s, q, jnp.zeros_like(q)) for s in sels)
        zero = jnp.zeros((SB_TQ, 1), F32)
        carries, acc = tile(qs, i, (zero, zero), jnp.zeros((SB_TQ, LANES), F32), 0)
        if static_prev:
            carries, acc = tile(qs, i - 1, carries, acc, None)

            def cond(state):
                return jnp.logical_and(state[0] >= 0, state[1])

            def body(state):
                kb, _, cs, acc_ = state
                cs, acc_ = tile(qs, kb, cs, acc_, None)
                return kb - 1, alive(cs), cs, acc_

            _, _, _, acc = lax.while_loop(cond, body, (i - 2, alive(carries), carries, acc))
        o_ref[pl.ds(r0, SB_TQ), :] = acc.astype(o_ref.dtype)

    qblock(0, False)

    def step(i, carry):
        qblock(i, True)
        return carry

    lax.fori_loop(1, seq // SB_TQ, step, 0)


def _sb(qkv, u, *, batch, seq, q_col, k_col, v_col):
    t = qkv.shape[0]
    pairs = SB_HEADS // HEADS_PER_TILE

    def col(c0):
        return lambda b, p: (b, c0 // LANES + p)

    return pl.pallas_call(
        _sb_kernel,
        grid=(batch, pairs),
        in_specs=[
            pl.BlockSpec((seq, LANES), col(q_col)),
            pl.BlockSpec((seq, LANES), col(k_col)),
            pl.BlockSpec((seq, LANES), col(v_col)),
            _resident(u.shape),
        ],
        out_specs=pl.BlockSpec((seq, LANES), lambda b, p: (b, p)),
        out_shape=jax.ShapeDtypeStruct((t, SB_HEADS * HEAD_DIM), BF16),
        compiler_params=pltpu.CompilerParams(
            dimension_semantics=("arbitrary", "arbitrary"), vmem_limit_bytes=VMEM_LIMIT),
        name="sb",
    )(qkv, qkv, qkv, u)


def _rel_bucket(dist):
    max_exact = REL_BUCKETS // 2
    d = jnp.maximum(dist, 1).astype(F32)
    large = max_exact + (jnp.log(d / max_exact) / math.log(REL_MAX_DIST / max_exact)
                         * (REL_BUCKETS - max_exact)).astype(jnp.int32)
    large = jnp.minimum(large, REL_BUCKETS - 1)
    return jnp.where(dist < max_exact, dist, large)


def _swa_bias(rel_table):
    f = rel_table.astype(F32)[_rel_bucket(jnp.arange(SWA_WINDOW))].T
    heads = f.shape[0]
    span = 3 * BLOCK
    v = jnp.pad(f, ((0, 0), (BLOCK - 1, span - SWA_WINDOW - (BLOCK - 1))))
    shifted = jnp.tile(v, (1, BLOCK + 1))[:, :BLOCK * (span + 1)].reshape(heads, BLOCK, span + 1)
    return shifted[:, :, :2 * BLOCK][:, :, ::-1]


def _band_mask():
    qi = np.arange(BLOCK)[:, None] + BLOCK
    kj = np.arange(2 * BLOCK)[None, :]
    dist = qi - kj
    return jnp.asarray(((dist >= 0) & (dist < SWA_WINDOW)).astype(np.float32))


def _dup_heads(w):
    d_in = w.shape[0]
    w = w.reshape(d_in, SWA_KV_HEADS, 1, HEAD_DIM)
    return jnp.broadcast_to(w, (d_in, SWA_KV_HEADS, HEADS_PER_TILE, HEAD_DIM)).reshape(d_in, -1)


def kernel(x, norm_ffn1, ffn1_w1, ffn1_w3, ffn1_w2, norm_mix, w_in, swa_sinks, rel_bias,
           w_branch_swa, w_branch_sb, w_out, norm_ffn2, ffn2_w1, ffn2_w3, ffn2_w2, norm_final):
    batch, seq, d = x.shape
    depth = norm_ffn1.shape[0]
    qa_w = SWA_Q_HEADS * HEAD_DIM
    kva_w = SWA_KV_HEADS * HEAD_DIM
    sb_w = SB_HEADS * HEAD_DIM
    scale = HEAD_DIM ** -0.5

    bias, band = _swa_bias(rel_bias), _band_mask()
    u = (jnp.arange(SB_TK)[:, None] >= jnp.arange(SB_TK)[None, :]).astype(BF16)
    gain_final = norm_final.reshape(1, d)
    kva_dup = SWA_KV_HEADS * LANES
    c_qa, c_ka, c_va = 0, qa_w, qa_w + kva_dup
    c_qb = qa_w + 2 * kva_dup
    c_kb, c_vb = c_qb + sb_w, c_qb + 2 * sb_w

    xt = x.reshape(batch * seq, d)
    for layer in range(depth):
        w = w_in[layer]
        o = 0
        cols = {}
        for name, width in (("qa", qa_w), ("ka", kva_w), ("va", kva_w), ("qb", sb_w), ("kb", sb_w),
                            ("vb", sb_w), ("g", 2 * d)):
            cols[name] = w[:, o:o + width]
            o += width
        wqkv = jnp.concatenate([cols["qa"] * scale, _dup_heads(cols["ka"]), _dup_heads(cols["va"]),
                                cols["qb"] * -scale, cols["kb"], cols["vb"]], axis=1).astype(BF16)
        x1, qkv, gate = _pre(xt, norm_ffn1[layer].reshape(1, d), ffn1_w1[layer].astype(BF16),
                             ffn1_w3[layer].astype(BF16), ffn1_w2[layer].astype(BF16),
                             norm_mix[layer].reshape(1, d), wqkv, cols["g"].astype(BF16))
        oa = _swa(qkv, swa_sinks[layer], bias, band, batch=batch, seq=seq,
                  q_col=c_qa, k_col=c_ka, v_col=c_va)
        ob = _sb(qkv, u, batch=batch, seq=seq, q_col=c_qb, k_col=c_kb, v_col=c_vb)
        xt = _post(x1, oa, ob, gate, w_branch_swa[layer].astype(BF16),
                   w_branch_sb[layer].astype(BF16), w_out[layer].astype(BF16),
                   norm_ffn2[layer].reshape(1, d), ffn2_w1[layer].astype(BF16),
                   ffn2_w3[layer].astype(BF16), ffn2_w2[layer].astype(BF16),
                   gain_final, final_norm=layer == depth - 1)
    return xt.reshape(batch, seq, d)
```

```python
import functools
import math

import jax
import jax.numpy as jnp
import numpy as np
from jax import lax
from jax.experimental import pallas as pl
from jax.experimental.pallas import tpu as pltpu

F32 = jnp.float32
BF16 = jnp.bfloat16

HEAD_DIM = 64
SWA_Q_HEADS = 8
SWA_KV_HEADS = 2
SWA_WINDOW = 128
SB_HEADS = 8
BLOCK = 128
REL_BUCKETS = 32
REL_MAX_DIST = 128
RMS_EPS = 1e-6
NEG_BIG = -1e30

LANES = 128
HEADS_PER_TILE = LANES // HEAD_DIM
FFN_CHUNK = 256
PROJ_CHUNK = 256
TOKEN_TILE = 512
SB_TQ = 256
SB_TK = 256
LOG2E = math.log2(math.e)
SB_LOG2_UNDERFLOW = -150.0
SWA_STEP = 1024
VMEM_LIMIT = 60000 * 1024


def _rms(x, g):
    ms = jnp.mean(x * x, axis=-1, keepdims=True)
    return x * lax.rsqrt(ms + RMS_EPS) * g


def _dot(a, b):
    return jnp.dot(a, b, preferred_element_type=F32)


def _dot_nt(a, b):
    return lax.dot_general(a, b, (((1,), (1,)), ((), ())), preferred_element_type=F32)


def _resident(shape):
    nd = len(shape)
    return pl.BlockSpec(shape, lambda *_: (0,) * nd, pipeline_mode=pl.Buffered(1))


def _swiglu_into(acc_ref, h, w1_ref, w3_ref, w2_ref):
    d_ff = w1_ref.shape[1]
    for c in range(d_ff // FFN_CHUNK):
        cols = slice(c * FFN_CHUNK, (c + 1) * FFN_CHUNK)
        a = _dot(h, w1_ref[:, cols])
        b = _dot(h, w3_ref[:, cols])
        gated = (a * jax.nn.sigmoid(a) * b).astype(BF16)
        down = _dot(gated, w2_ref[cols, :])
        if c == 0:
            acc_ref[...] = down
        else:
            acc_ref[...] += down


def _pre_kernel(x_ref, g1_ref, w1_ref, w3_ref, w2_ref, gm_ref, wqkv_ref, wg_ref,
                x1_ref, qkv_ref, gate_ref, acc_ref):
    x = x_ref[...]
    _swiglu_into(acc_ref, _rms(x, g1_ref[...]).astype(BF16), w1_ref, w3_ref, w2_ref)
    x1 = x + 0.5 * acc_ref[...]
    x1_ref[...] = x1
    h = _rms(x1, gm_ref[...]).astype(BF16)
    for c in range(qkv_ref.shape[1] // PROJ_CHUNK):
        cols = slice(c * PROJ_CHUNK, (c + 1) * PROJ_CHUNK)
        qkv_ref[:, cols] = _dot(h, wqkv_ref[:, cols]).astype(BF16)
    for c in range(gate_ref.shape[1] // PROJ_CHUNK):
        cols = slice(c * PROJ_CHUNK, (c + 1) * PROJ_CHUNK)
        gate_ref[:, cols] = jax.nn.sigmoid(_dot(h, wg_ref[:, cols])).astype(BF16)


def _pre(x, g1, w1, w3, w2, gm, wqkv, wg):
    t, d = x.shape
    tm = TOKEN_TILE
    row = lambda i: (i, 0)
    return pl.pallas_call(
        _pre_kernel,
        grid=(t // tm,),
        in_specs=[pl.BlockSpec((tm, d), row), _resident(g1.shape), _resident(w1.shape),
                  _resident(w3.shape), _resident(w2.shape), _resident(gm.shape),
                  _resident(wqkv.shape), _resident(wg.shape)],
        out_specs=[pl.BlockSpec((tm, d), row), pl.BlockSpec((tm, wqkv.shape[1]), row),
                   pl.BlockSpec((tm, wg.shape[1]), row)],
        out_shape=[jax.ShapeDtypeStruct((t, d), F32),
                   jax.ShapeDtypeStruct((t, wqkv.shape[1]), BF16),
                   jax.ShapeDtypeStruct((t, wg.shape[1]), BF16)],
        scratch_shapes=[pltpu.VMEM((tm, d), F32)],
        compiler_params=pltpu.CompilerParams(
            dimension_semantics=("arbitrary",), vmem_limit_bytes=VMEM_LIMIT),
        name="pre",
    )(x, g1, w1, w3, w2, gm, wqkv, wg)


def _post_kernel(x_ref, oa_ref, ob_ref, gate_ref, wa_ref, wb_ref, wo_ref, g2_ref,
                 w1_ref, w3_ref, w2_ref, gf_ref, o_ref, acc_ref, mrg_ref, *, final_norm):
    d = x_ref.shape[1]
    oa = oa_ref[...]
    ob = ob_ref[...]
    for c in range(d // PROJ_CHUNK):
        cols = slice(c * PROJ_CHUNK, (c + 1) * PROJ_CHUNK)
        gcols = slice(d + c * PROJ_CHUNK, d + (c + 1) * PROJ_CHUNK)
        merged = (gate_ref[:, cols].astype(F32) * _dot(oa, wa_ref[:, cols])
                  + gate_ref[:, gcols].astype(F32) * _dot(ob, wb_ref[:, cols]))
        mrg_ref[:, cols] = merged.astype(BF16)
    x2 = x_ref[...] + _dot(mrg_ref[...], wo_ref[...])
    _swiglu_into(acc_ref, _rms(x2, g2_ref[...]).astype(BF16), w1_ref, w3_ref, w2_ref)
    y = x2 + 0.5 * acc_ref[...]
    if final_norm:
        y = _rms(y, gf_ref[...])
    o_ref[...] = y


def _post(x, oa, ob, gate, wa, wb, wo, g2, w1, w3, w2, gf, *, final_norm):
    t, d = x.shape
    tm = TOKEN_TILE
    row = lambda i: (i, 0)
    return pl.pallas_call(
        functools.partial(_post_kernel, final_norm=final_norm),
        grid=(t // tm,),
        in_specs=[pl.BlockSpec((tm, d), row), pl.BlockSpec((tm, oa.shape[1]), row),
                  pl.BlockSpec((tm, ob.shape[1]), row), pl.BlockSpec((tm, gate.shape[1]), row),
                  _resident(wa.shape), _resident(wb.shape), _resident(wo.shape),
                  _resident(g2.shape), _resident(w1.shape), _resident(w3.shape),
                  _resident(w2.shape), _resident(gf.shape)],
        out_specs=pl.BlockSpec((tm, d), row),
        out_shape=jax.ShapeDtypeStruct((t, d), F32),
        scratch_shapes=[pltpu.VMEM((tm, d), F32), pltpu.VMEM((tm, d), BF16)],
        compiler_params=pltpu.CompilerParams(
            dimension_semantics=("arbitrary",), vmem_limit_bytes=VMEM_LIMIT),
        name="post",
    )(x, oa, ob, gate, wa, wb, wo, g2, w1, w3, w2, gf)


def _swa_kernel(sink_ref, q_ref, kprev_ref, k_ref, vprev_ref, v_ref, bias_ref, band_ref,
                o_ref, kwin_ref, vwin_ref):
    j = pl.program_id(1)
    ts = q_ref.shape[0]
    kwin_ref[0:BLOCK, :] = kprev_ref[...]
    kwin_ref[BLOCK:, :] = k_ref[...]
    vwin_ref[0:BLOCK, :] = vprev_ref[...]
    vwin_ref[BLOCK:, :] = v_ref[...]
    lane = lax.broadcasted_iota(jnp.int32, (1, LANES), 1)
    low_half = lane < HEAD_DIM
    key_col = lax.broadcasted_iota(jnp.int32, (1, 2 * BLOCK), 1)
    band = band_ref[...] > 0.0
    group = SWA_Q_HEADS // SWA_KV_HEADS

    def block(n, carry):
        r0 = pl.multiple_of(n * BLOCK, BLOCK)
        first = jnp.logical_and(j == 0, n == 0)
        valid = jnp.logical_and(band, jnp.logical_or(key_col >= BLOCK, jnp.logical_not(first)))
        heads = range(SWA_Q_HEADS)
        sels = (low_half, jnp.logical_not(low_half))
        kws = [kwin_ref[pl.ds(r0, 2 * BLOCK), g * LANES:(g + 1) * LANES]
               for g in range(SWA_KV_HEADS)]
        vws = [vwin_ref[pl.ds(r0, 2 * BLOCK), g * LANES:(g + 1) * LANES]
               for g in range(SWA_KV_HEADS)]
        logits = []
        for hd in heads:
            p, e = divmod(hd, HEADS_PER_TILE)
            q = q_ref[pl.ds(r0, BLOCK), p * LANES:(p + 1) * LANES]
            logits.append(_dot_nt(jnp.where(sels[e], q, jnp.zeros_like(q)), kws[hd // group]))
        probs, inv = [], []
        for hd in heads:
            lg = jnp.where(valid, logits[hd] + bias_ref[hd], NEG_BIG)
            sink = sink_ref[hd]
            m = jnp.maximum(jnp.max(lg, axis=-1, keepdims=True), sink)
            pr = jnp.exp(lg - m)
            inv.append(1.0 / (jnp.sum(pr, axis=-1, keepdims=True) + jnp.exp(sink - m)))
            probs.append(pr.astype(BF16))
        for p in range(SWA_Q_HEADS // HEADS_PER_TILE):
            acc = jnp.zeros((BLOCK, LANES), F32)
            for e in range(HEADS_PER_TILE):
                hd = p * HEADS_PER_TILE + e
                vw = vws[hd // group]
                acc = acc + _dot(probs[hd], jnp.where(sels[e], vw, jnp.zeros_like(vw)))
            o_ref[pl.ds(r0, BLOCK), p * LANES:(p + 1) * LANES] = (
                acc * jnp.where(low_half, inv[p * HEADS_PER_TILE], inv[p * HEADS_PER_TILE + 1])
            ).astype(o_ref.dtype)
        return carry

    lax.fori_loop(0, ts // BLOCK, block, 0, unroll=2)


def _swa(qkv, sinks, bias, band, *, batch, seq, q_col, k_col, v_col):
    t = qkv.shape[0]
    ts = min(SWA_STEP, seq)
    steps = seq // ts
    bps = ts // BLOCK
    nblk = seq // BLOCK
    qw = SWA_Q_HEADS * HEAD_DIM
    kvw = SWA_KV_HEADS * LANES

    def main(col_blocks):
        return lambda b, j: (b * steps + j, col_blocks)

    def prev(col_blocks):
        return lambda b, j: (b * nblk + jnp.maximum(j * bps - 1, 0), col_blocks)

    return pl.pallas_call(
        _swa_kernel,
        grid=(batch, steps),
        in_specs=[
            pl.BlockSpec(memory_space=pltpu.SMEM),
            pl.BlockSpec((ts, qw), main(q_col // qw)),
            pl.BlockSpec((BLOCK, kvw), prev(k_col // kvw)),
            pl.BlockSpec((ts, kvw), main(k_col // kvw)),
            pl.BlockSpec((BLOCK, kvw), prev(v_col // kvw)),
            pl.BlockSpec((ts, kvw), main(v_col // kvw)),
            _resident(bias.shape),
            _resident(band.shape),
        ],
        out_specs=pl.BlockSpec((ts, qw), lambda b, j: (b * steps + j, 0)),
        out_shape=jax.ShapeDtypeStruct((t, qw), BF16),
        scratch_shapes=[pltpu.VMEM((ts + BLOCK, kvw), BF16), pltpu.VMEM((ts + BLOCK, kvw), BF16)],
        compiler_params=pltpu.CompilerParams(
            dimension_semantics=("arbitrary", "arbitrary"), vmem_limit_bytes=VMEM_LIMIT),
        name="swa",
    )(sinks, qkv, qkv, qkv, qkv, qkv, bias, band)


def _sb_kernel(q_ref, k_ref, v_ref, u_ref, o_ref):
    seq = q_ref.shape[0]
    lane = lax.broadcasted_iota(jnp.int32, (1, LANES), 1)
    low_half = lane < HEAD_DIM
    sels = (low_half, jnp.logical_not(low_half))
    rows = lax.broadcasted_iota(jnp.int32, (SB_TQ, SB_TK), 0)
    cols = lax.broadcasted_iota(jnp.int32, (SB_TQ, SB_TK), 1)
    col_minus_row = cols - rows

    def tiles(qs, blocks, carries, acc):
        u = u_ref[...]
        pairs = [(t, e) for t in range(len(blocks)) for e in range(HEADS_PER_TILE)]
        ks, vs = [], []
        for kb, _ in blocks:
            start = pl.multiple_of(kb * SB_TK, SB_TK)
            ks.append(k_ref[pl.ds(start, SB_TK), :])
            vs.append(v_ref[pl.ds(start, SB_TK), :])
        w2 = {(t, e): _dot_nt(qs[e], ks[t]) * LOG2E for t, e in pairs}
        hilo = {}
        for t, e in pairs:
            if blocks[t][1] is not None:
                w2[t, e] = jnp.where(col_minus_row < blocks[t][1], w2[t, e], -NEG_BIG)
            neg_abs = lax.bitcast_convert_type(
                lax.bitcast_convert_type(w2[t, e], jnp.uint32) | jnp.uint32(0x80000000), F32)
            log_keep = jnp.minimum(w2[t, e], 0.0) - jnp.log2(1.0 + jnp.exp2(neg_abs))
            hi = log_keep.astype(BF16)
            lo = (log_keep - hi.astype(F32)).astype(BF16)
            hilo[t, e] = jnp.concatenate([hi, lo], axis=1)
        sums = {p: _dot(hilo[p], u) for p in pairs}
        carries = list(carries)
        weights = {}
        for t, e in pairs:
            c = sums[t, e] + carries[e]
            weights[t, e] = jnp.exp2(c - w2[t, e]).astype(BF16)
            carries[e] = c[:, 0:1]
        for t, e in pairs:
            acc = acc + _dot(weights[t, e], jnp.where(sels[e], vs[t], jnp.zeros_like(vs[t])))
        return tuple(carries), acc

    def alive(carries):
        return jnp.max(jnp.maximum(carries[0], carries[1])) > SB_LOG2_UNDERFLOW

    def qblock(i, static_prev):
        r0 = pl.multiple_of(i * SB_TQ, SB_TQ)
        q = q_ref[pl.ds(r0, SB_TQ), :]
        qs = tuple(jnp.where(s, q, jnp.zeros_like(q)) for s in sels)
        zero = jnp.zeros((SB_TQ, 1), F32)
        blocks = [(i, 0), (i - 1, None)] if static_prev else [(i, 0)]
        carries, acc = tiles(qs, blocks, (zero, zero), jnp.zeros((SB_TQ, LANES), F32))
        if static_prev:

            def cond(state):
                return jnp.logical_and(state[0] >= 0, state[1])

            def body(state):
                kb, _, cs, acc_ = state
                cs, acc_ = tiles(qs, [(kb, None)], cs, acc_)
                return kb - 1, alive(cs), cs, acc_

            _, _, _, acc = lax.while_loop(cond, body, (i - 2, alive(carries), carries, acc))
        o_ref[pl.ds(r0, SB_TQ), :] = acc.astype(o_ref.dtype)

    qblock(0, False)

    def step(i, carry):
        qblock(i, True)
        return carry

    lax.fori_loop(1, seq // SB_TQ, step, 0)


def _sb(qkv, u, *, batch, seq, q_col, k_col, v_col):
    t = qkv.shape[0]
    pairs = SB_HEADS // HEADS_PER_TILE

    def col(c0):
        return lambda b, p: (b, c0 // LANES + p)

    return pl.pallas_call(
        _sb_kernel,
        grid=(batch, pairs),
        in_specs=[
            pl.BlockSpec((seq, LANES), col(q_col)),
            pl.BlockSpec((seq, LANES), col(k_col)),
            pl.BlockSpec((seq, LANES), col(v_col)),
            _resident(u.shape),
        ],
        out_specs=pl.BlockSpec((seq, LANES), lambda b, p: (b, p)),
        out_shape=jax.ShapeDtypeStruct((t, SB_HEADS * HEAD_DIM), BF16),
        compiler_params=pltpu.CompilerParams(
            dimension_semantics=("arbitrary", "arbitrary"), vmem_limit_bytes=VMEM_LIMIT),
        name="sb",
    )(qkv, qkv, qkv, u)


def _rel_bucket(dist):
    max_exact = REL_BUCKETS // 2
    d = jnp.maximum(dist, 1).astype(F32)
    large = max_exact + (jnp.log(d / max_exact) / math.log(REL_MAX_DIST / max_exact)
                         * (REL_BUCKETS - max_exact)).astype(jnp.int32)
    large = jnp.minimum(large, REL_BUCKETS - 1)
    return jnp.where(dist < max_exact, dist, large)


def _swa_bias(rel_table):
    f = rel_table.astype(F32)[_rel_bucket(jnp.arange(SWA_WINDOW))].T
    heads = f.shape[0]
    span = 3 * BLOCK
    v = jnp.pad(f, ((0, 0), (BLOCK - 1, span - SWA_WINDOW - (BLOCK - 1))))
    shifted = jnp.tile(v, (1, BLOCK + 1))[:, :BLOCK * (span + 1)].reshape(heads, BLOCK, span + 1)
    return shifted[:, :, :2 * BLOCK][:, :, ::-1]


def _band_mask():
    qi = np.arange(BLOCK)[:, None] + BLOCK
    kj = np.arange(2 * BLOCK)[None, :]
    dist = qi - kj
    return jnp.asarray(((dist >= 0) & (dist < SWA_WINDOW)).astype(np.float32))


def _dup_heads(w):
    d_in = w.shape[0]
    w = w.reshape(d_in, SWA_KV_HEADS, 1, HEAD_DIM)
    return jnp.broadcast_to(w, (d_in, SWA_KV_HEADS, HEADS_PER_TILE, HEAD_DIM)).reshape(d_in, -1)


def kernel(x, norm_ffn1, ffn1_w1, ffn1_w3, ffn1_w2, norm_mix, w_in, swa_sinks, rel_bias,
           w_branch_swa, w_branch_sb, w_out, norm_ffn2, ffn2_w1, ffn2_w3, ffn2_w2, norm_final):
    batch, seq, d = x.shape
    depth = norm_ffn1.shape[0]
    qa_w = SWA_Q_HEADS * HEAD_DIM
    kva_w = SWA_KV_HEADS * HEAD_DIM
    sb_w = SB_HEADS * HEAD_DIM
    scale = HEAD_DIM ** -0.5

    bias, band = _swa_bias(rel_bias), _band_mask()
    tri = (jnp.arange(SB_TK)[:, None] >= jnp.arange(SB_TK)[None, :]).astype(BF16)
    u = jnp.concatenate([tri, tri], axis=0)
    gain_final = norm_final.reshape(1, d)
    kva_dup = SWA_KV_HEADS * LANES
    c_qa, c_ka, c_va = 0, qa_w, qa_w + kva_dup
    c_qb = qa_w + 2 * kva_dup
    c_kb, c_vb = c_qb + sb_w, c_qb + 2 * sb_w

    xt = x.reshape(batch * seq, d)
    for layer in range(depth):
        w = w_in[layer]
        o = 0
        cols = {}
        for name, width in (("qa", qa_w), ("ka", kva_w), ("va", kva_w), ("qb", sb_w), ("kb", sb_w),
                            ("vb", sb_w), ("g", 2 * d)):
            cols[name] = w[:, o:o + width]
            o += width
        wqkv = jnp.concatenate([cols["qa"] * scale, _dup_heads(cols["ka"]), _dup_heads(cols["va"]),
                                cols["qb"] * -scale, cols["kb"], cols["vb"]], axis=1).astype(BF16)
        x1, qkv, gate = _pre(xt, norm_ffn1[layer].reshape(1, d), ffn1_w1[layer].astype(BF16),
                             ffn1_w3[layer].astype(BF16), ffn1_w2[layer].astype(BF16),
                             norm_mix[layer].reshape(1, d), wqkv, cols["g"].astype(BF16))
        oa = _swa(qkv, swa_sinks[layer], bias, band, batch=batch, seq=seq,
                  q_col=c_qa, k_col=c_ka, v_col=c_va)
        ob = _sb(qkv, u, batch=batch, seq=seq, q_col=c_qb, k_col=c_kb, v_col=c_vb)
        xt = _post(x1, oa, ob, gate, w_branch_swa[layer].astype(BF16),
                   w_branch_sb[layer].astype(BF16), w_out[layer].astype(BF16),
                   norm_ffn2[layer].reshape(1, d), ffn2_w1[layer].astype(BF16),
                   ffn2_w3[layer].astype(BF16), ffn2_w2[layer].astype(BF16),
                   gain_final, final_norm=layer == depth - 1)
    return xt.reshape(batch, seq, d)
```

```python
import functools
import math

import jax
import jax.numpy as jnp
import numpy as np
from jax import lax
from jax.experimental import pallas as pl
from jax.experimental.pallas import tpu as pltpu

F32 = jnp.float32
BF16 = jnp.bfloat16

HEAD_DIM = 64
SWA_Q_HEADS = 8
SWA_KV_HEADS = 2
SWA_WINDOW = 128
SB_HEADS = 8
BLOCK = 128
REL_BUCKETS = 32
REL_MAX_DIST = 128
RMS_EPS = 1e-6
NEG_BIG = -1e30

LANES = 128
HEADS_PER_TILE = LANES // HEAD_DIM
FFN_CHUNK = 256
PROJ_CHUNK = 256
TOKEN_TILE = 512
SB_TQ = 256
SB_TK = 256
SB_GROUP = 2
LOG2E = math.log2(math.e)
SB_LOG2_UNDERFLOW = -150.0
SWA_STEP = 1024
VMEM_LIMIT = 60000 * 1024


def _rms(x, g):
    ms = jnp.mean(x * x, axis=-1, keepdims=True)
    return x * lax.rsqrt(ms + RMS_EPS) * g


def _dot(a, b):
    return jnp.dot(a, b, preferred_element_type=F32)


def _dot_nt(a, b):
    return lax.dot_general(a, b, (((1,), (1,)), ((), ())), preferred_element_type=F32)


def _resident(shape):
    nd = len(shape)
    return pl.BlockSpec(shape, lambda *_: (0,) * nd, pipeline_mode=pl.Buffered(1))


def _swiglu_into(acc_ref, h, w1_ref, w3_ref, w2_ref):
    d_ff = w1_ref.shape[1]
    for c in range(d_ff // FFN_CHUNK):
        cols = slice(c * FFN_CHUNK, (c + 1) * FFN_CHUNK)
        a = _dot(h, w1_ref[:, cols])
        b = _dot(h, w3_ref[:, cols])
        gated = (a * jax.nn.sigmoid(a) * b).astype(BF16)
        down = _dot(gated, w2_ref[cols, :])
        if c == 0:
            acc_ref[...] = down
        else:
            acc_ref[...] += down


def _pre_kernel(x_ref, g1_ref, w1_ref, w3_ref, w2_ref, gm_ref, wqkv_ref, wg_ref,
                x1_ref, qkv_ref, gate_ref, acc_ref):
    x = x_ref[...]
    _swiglu_into(acc_ref, _rms(x, g1_ref[...]).astype(BF16), w1_ref, w3_ref, w2_ref)
    x1 = x + 0.5 * acc_ref[...]
    x1_ref[...] = x1
    h = _rms(x1, gm_ref[...]).astype(BF16)
    for c in range(qkv_ref.shape[1] // PROJ_CHUNK):
        cols = slice(c * PROJ_CHUNK, (c + 1) * PROJ_CHUNK)
        qkv_ref[:, cols] = _dot(h, wqkv_ref[:, cols]).astype(BF16)
    for c in range(gate_ref.shape[1] // PROJ_CHUNK):
        cols = slice(c * PROJ_CHUNK, (c + 1) * PROJ_CHUNK)
        gate_ref[:, cols] = jax.nn.sigmoid(_dot(h, wg_ref[:, cols])).astype(BF16)


def _pre(x, g1, w1, w3, w2, gm, wqkv, wg):
    t, d = x.shape
    tm = TOKEN_TILE
    row = lambda i: (i, 0)
    return pl.pallas_call(
        _pre_kernel,
        grid=(t // tm,),
        in_specs=[pl.BlockSpec((tm, d), row), _resident(g1.shape), _resident(w1.shape),
                  _resident(w3.shape), _resident(w2.shape), _resident(gm.shape),
                  _resident(wqkv.shape), _resident(wg.shape)],
        out_specs=[pl.BlockSpec((tm, d), row), pl.BlockSpec((tm, wqkv.shape[1]), row),
                   pl.BlockSpec((tm, wg.shape[1]), row)],
        out_shape=[jax.ShapeDtypeStruct((t, d), F32),
                   jax.ShapeDtypeStruct((t, wqkv.shape[1]), BF16),
                   jax.ShapeDtypeStruct((t, wg.shape[1]), BF16)],
        scratch_shapes=[pltpu.VMEM((tm, d), F32)],
        compiler_params=pltpu.CompilerParams(
            dimension_semantics=("arbitrary",), vmem_limit_bytes=VMEM_LIMIT),
        name="pre",
    )(x, g1, w1, w3, w2, gm, wqkv, wg)


def _post_kernel(x_ref, oa_ref, ob_ref, gate_ref, wa_ref, wb_ref, wo_ref, g2_ref,
                 w1_ref, w3_ref, w2_ref, gf_ref, o_ref, acc_ref, mrg_ref, *, final_norm):
    d = x_ref.shape[1]
    oa = oa_ref[...]
    ob = ob_ref[...]
    for c in range(d // PROJ_CHUNK):
        cols = slice(c * PROJ_CHUNK, (c + 1) * PROJ_CHUNK)
        gcols = slice(d + c * PROJ_CHUNK, d + (c + 1) * PROJ_CHUNK)
        merged = (gate_ref[:, cols].astype(F32) * _dot(oa, wa_ref[:, cols])
                  + gate_ref[:, gcols].astype(F32) * _dot(ob, wb_ref[:, cols]))
        mrg_ref[:, cols] = merged.astype(BF16)
    x2 = x_ref[...] + _dot(mrg_ref[...], wo_ref[...])
    _swiglu_into(acc_ref, _rms(x2, g2_ref[...]).astype(BF16), w1_ref, w3_ref, w2_ref)
    y = x2 + 0.5 * acc_ref[...]
    if final_norm:
        y = _rms(y, gf_ref[...])
    o_ref[...] = y


def _post(x, oa, ob, gate, wa, wb, wo, g2, w1, w3, w2, gf, *, final_norm):
    t, d = x.shape
    tm = TOKEN_TILE
    row = lambda i: (i, 0)
    return pl.pallas_call(
        functools.partial(_post_kernel, final_norm=final_norm),
        grid=(t // tm,),
        in_specs=[pl.BlockSpec((tm, d), row), pl.BlockSpec((tm, oa.shape[1]), row),
                  pl.BlockSpec((tm, ob.shape[1]), row), pl.BlockSpec((tm, gate.shape[1]), row),
                  _resident(wa.shape), _resident(wb.shape), _resident(wo.shape),
                  _resident(g2.shape), _resident(w1.shape), _resident(w3.shape),
                  _resident(w2.shape), _resident(gf.shape)],
        out_specs=pl.BlockSpec((tm, d), row),
        out_shape=jax.ShapeDtypeStruct((t, d), F32),
        scratch_shapes=[pltpu.VMEM((tm, d), F32), pltpu.VMEM((tm, d), BF16)],
        compiler_params=pltpu.CompilerParams(
            dimension_semantics=("arbitrary",), vmem_limit_bytes=VMEM_LIMIT),
        name="post",
    )(x, oa, ob, gate, wa, wb, wo, g2, w1, w3, w2, gf)


def _swa_kernel(sink_ref, q_ref, kprev_ref, k_ref, vprev_ref, v_ref, bias_ref, band_ref,
                o_ref, kwin_ref, vwin_ref):
    j = pl.program_id(1)
    ts = q_ref.shape[0]
    kwin_ref[0:BLOCK, :] = kprev_ref[...]
    kwin_ref[BLOCK:, :] = k_ref[...]
    vwin_ref[0:BLOCK, :] = vprev_ref[...]
    vwin_ref[BLOCK:, :] = v_ref[...]
    lane = lax.broadcasted_iota(jnp.int32, (1, LANES), 1)
    low_half = lane < HEAD_DIM
    key_col = lax.broadcasted_iota(jnp.int32, (1, 2 * BLOCK), 1)
    band = band_ref[...] > 0.0
    group = SWA_Q_HEADS // SWA_KV_HEADS
    heads = range(SWA_Q_HEADS)
    sels = (low_half, jnp.logical_not(low_half))
    fills = [jnp.where(key_col == 0, sink_ref[hd], NEG_BIG) for hd in heads]
    not_row0 = lax.broadcasted_iota(jnp.int32, (2 * BLOCK, 1), 0) != 0
    vsels = tuple(jnp.logical_and(not_row0, s) for s in sels)

    def block(n, carry):
        r0 = pl.multiple_of(n * BLOCK, BLOCK)
        first = jnp.logical_and(j == 0, n == 0)
        valid = jnp.logical_and(band, jnp.logical_or(key_col >= BLOCK, jnp.logical_not(first)))
        kws = [kwin_ref[pl.ds(r0, 2 * BLOCK), g * LANES:(g + 1) * LANES]
               for g in range(SWA_KV_HEADS)]
        vws = [vwin_ref[pl.ds(r0, 2 * BLOCK), g * LANES:(g + 1) * LANES]
               for g in range(SWA_KV_HEADS)]
        logits = []
        for hd in heads:
            p, e = divmod(hd, HEADS_PER_TILE)
            q = q_ref[pl.ds(r0, BLOCK), p * LANES:(p + 1) * LANES]
            logits.append(_dot_nt(jnp.where(sels[e], q, jnp.zeros_like(q)), kws[hd // group]))
        probs, inv = [], []
        for hd in heads:
            lg = jnp.where(valid, logits[hd] + bias_ref[hd], fills[hd])
            pr = jnp.exp(lg - jnp.max(lg, axis=-1, keepdims=True))
            inv.append(1.0 / jnp.sum(pr, axis=-1, keepdims=True))
            probs.append(pr.astype(BF16))
        for p in range(SWA_Q_HEADS // HEADS_PER_TILE):
            acc = jnp.zeros((BLOCK, LANES), F32)
            for e in range(HEADS_PER_TILE):
                hd = p * HEADS_PER_TILE + e
                vw = vws[hd // group]
                acc = acc + _dot(probs[hd], jnp.where(vsels[e], vw, jnp.zeros_like(vw)))
            o_ref[pl.ds(r0, BLOCK), p * LANES:(p + 1) * LANES] = (
                acc * jnp.where(low_half, inv[p * HEADS_PER_TILE], inv[p * HEADS_PER_TILE + 1])
            ).astype(o_ref.dtype)
        return carry

    lax.fori_loop(0, ts // BLOCK, block, 0, unroll=2)


def _swa(qkv, sinks, bias, band, *, batch, seq, q_col, k_col, v_col):
    t = qkv.shape[0]
    ts = min(SWA_STEP, seq)
    steps = seq // ts
    bps = ts // BLOCK
    nblk = seq // BLOCK
    qw = SWA_Q_HEADS * HEAD_DIM
    kvw = SWA_KV_HEADS * LANES

    def main(col_blocks):
        return lambda b, j: (b * steps + j, col_blocks)

    def prev(col_blocks):
        return lambda b, j: (b * nblk + jnp.maximum(j * bps - 1, 0), col_blocks)

    return pl.pallas_call(
        _swa_kernel,
        grid=(batch, steps),
        in_specs=[
            pl.BlockSpec(memory_space=pltpu.SMEM),
            pl.BlockSpec((ts, qw), main(q_col // qw)),
            pl.BlockSpec((BLOCK, kvw), prev(k_col // kvw)),
            pl.BlockSpec((ts, kvw), main(k_col // kvw)),
            pl.BlockSpec((BLOCK, kvw), prev(v_col // kvw)),
            pl.BlockSpec((ts, kvw), main(v_col // kvw)),
            _resident(bias.shape),
            _resident(band.shape),
        ],
        out_specs=pl.BlockSpec((ts, qw), lambda b, j: (b * steps + j, 0)),
        out_shape=jax.ShapeDtypeStruct((t, qw), BF16),
        scratch_shapes=[pltpu.VMEM((ts + BLOCK, kvw), BF16), pltpu.VMEM((ts + BLOCK, kvw), BF16)],
        compiler_params=pltpu.CompilerParams(
            dimension_semantics=("arbitrary", "arbitrary"), vmem_limit_bytes=VMEM_LIMIT),
        name="swa",
    )(sinks, qkv, qkv, qkv, qkv, qkv, bias, band)


def _sb_kernel(q_ref, k_ref, v_ref, u_ref, o_ref):
    seq = q_ref.shape[0]
    lane = lax.broadcasted_iota(jnp.int32, (1, LANES), 1)
    low_half = lane < HEAD_DIM
    sels = (low_half, jnp.logical_not(low_half))
    rows = lax.broadcasted_iota(jnp.int32, (SB_TQ, SB_TK), 0)
    cols = lax.broadcasted_iota(jnp.int32, (SB_TQ, SB_TK), 1)
    col_minus_row = cols - rows

    def sweep(items):
        u = u_ref[...]
        work = [(n, t, e) for n, item in enumerate(items)
                for t in range(len(item[1])) for e in range(HEADS_PER_TILE)]
        ks, vs = {}, {}
        for n, (_, blocks, _, _) in enumerate(items):
            for t, (kb, _) in enumerate(blocks):
                start = pl.multiple_of(kb * SB_TK, SB_TK)
                ks[n, t] = k_ref[pl.ds(start, SB_TK), :]
                vs[n, t] = v_ref[pl.ds(start, SB_TK), :]
        w2 = {(n, t, e): _dot_nt(items[n][0][e], ks[n, t]) * LOG2E for n, t, e in work}
        hilo = {}
        for n, t, e in work:
            off = items[n][1][t][1]
            if off is not None:
                w2[n, t, e] = jnp.where(col_minus_row < off, w2[n, t, e], -NEG_BIG)
            neg_abs = lax.bitcast_convert_type(
                lax.bitcast_convert_type(w2[n, t, e], jnp.uint32) | jnp.uint32(0x80000000), F32)
            log_keep = jnp.minimum(w2[n, t, e], 0.0) - jnp.log2(1.0 + jnp.exp2(neg_abs))
            hi = log_keep.astype(BF16)
            lo = (log_keep - hi.astype(F32)).astype(BF16)
            hilo[n, t, e] = jnp.concatenate([hi, lo], axis=1)
        sums = {w: _dot(hilo[w], u) for w in work}
        carries = [list(item[2]) for item in items]
        weights = {}
        for n, t, e in work:
            c = sums[n, t, e] + carries[n][e]
            weights[n, t, e] = jnp.exp2(c - w2[n, t, e]).astype(BF16)
            carries[n][e] = c[:, 0:1]
        accs = [item[3] for item in items]
        for n, t, e in work:
            v = vs[n, t]
            accs[n] = accs[n] + _dot(weights[n, t, e], jnp.where(sels[e], v, jnp.zeros_like(v)))
        return [(tuple(c), a) for c, a in zip(carries, accs)]

    def alive(carries):
        return jnp.max(jnp.maximum(carries[0], carries[1])) > SB_LOG2_UNDERFLOW

    def qblocks(idx, static_prev):
        zero = jnp.zeros((SB_TQ, 1), F32)
        items = []
        for i in idx:
            q = q_ref[pl.ds(pl.multiple_of(i * SB_TQ, SB_TQ), SB_TQ), :]
            qs = tuple(jnp.where(s, q, jnp.zeros_like(q)) for s in sels)
            blocks = [(i, 0), (i - 1, None)] if static_prev else [(i, 0)]
            items.append((qs, blocks, (zero, zero), jnp.zeros((SB_TQ, LANES), F32)))
        for i, (qs, _, _, _), (carries, acc) in zip(idx, items, sweep(items)):
            if static_prev:

                def cond(state):
                    return jnp.logical_and(state[0] >= 0, state[1])

                def body(state, qs=qs):
                    kb, _, cs, acc_ = state
                    (cs, acc_), = sweep([(qs, [(kb, None)], cs, acc_)])
                    return kb - 1, alive(cs), cs, acc_

                _, _, _, acc = lax.while_loop(cond, body, (i - 2, alive(carries), carries, acc))
            o_ref[pl.ds(pl.multiple_of(i * SB_TQ, SB_TQ), SB_TQ), :] = acc.astype(o_ref.dtype)

    n_q = seq // SB_TQ
    qblocks([0], False)

    def step(j, carry):
        i = 1 + SB_GROUP * j
        qblocks([i + g for g in range(SB_GROUP)], True)
        return carry

    n_groups = (n_q - 1) // SB_GROUP
    lax.fori_loop(0, n_groups, step, 0)
    for i in range(1 + n_groups * SB_GROUP, n_q):
        qblocks([i], True)


def _sb(qkv, u, *, batch, seq, q_col, k_col, v_col):
    t = qkv.shape[0]
    pairs = SB_HEADS // HEADS_PER_TILE

    def col(c0):
        return lambda b, p: (b, c0 // LANES + p)

    return pl.pallas_call(
        _sb_kernel,
        grid=(batch, pairs),
        in_specs=[
            pl.BlockSpec((seq, LANES), col(q_col)),
            pl.BlockSpec((seq, LANES), col(k_col)),
            pl.BlockSpec((seq, LANES), col(v_col)),
            _resident(u.shape),
        ],
        out_specs=pl.BlockSpec((seq, LANES), lambda b, p: (b, p)),
        out_shape=jax.ShapeDtypeStruct((t, SB_HEADS * HEAD_DIM), BF16),
        compiler_params=pltpu.CompilerParams(
            dimension_semantics=("arbitrary", "arbitrary"), vmem_limit_bytes=VMEM_LIMIT),
        name="sb",
    )(qkv, qkv, qkv, u)


def _rel_bucket(dist):
    max_exact = REL_BUCKETS // 2
    d = jnp.maximum(dist, 1).astype(F32)
    large = max_exact + (jnp.log(d / max_exact) / math.log(REL_MAX_DIST / max_exact)
                         * (REL_BUCKETS - max_exact)).astype(jnp.int32)
    large = jnp.minimum(large, REL_BUCKETS - 1)
    return jnp.where(dist < max_exact, dist, large)


def _swa_bias(rel_table):
    f = rel_table.astype(F32)[_rel_bucket(jnp.arange(SWA_WINDOW))].T
    heads = f.shape[0]
    span = 3 * BLOCK
    v = jnp.pad(f, ((0, 0), (BLOCK - 1, span - SWA_WINDOW - (BLOCK - 1))))
    shifted = jnp.tile(v, (1, BLOCK + 1))[:, :BLOCK * (span + 1)].reshape(heads, BLOCK, span + 1)
    return shifted[:, :, :2 * BLOCK][:, :, ::-1]


def _band_mask():
    qi = np.arange(BLOCK)[:, None] + BLOCK
    kj = np.arange(2 * BLOCK)[None, :]
    dist = qi - kj
    return jnp.asarray(((dist >= 0) & (dist < SWA_WINDOW)).astype(np.float32))


def _dup_heads(w):
    d_in = w.shape[0]
    w = w.reshape(d_in, SWA_KV_HEADS, 1, HEAD_DIM)
    return jnp.broadcast_to(w, (d_in, SWA_KV_HEADS, HEADS_PER_TILE, HEAD_DIM)).reshape(d_in, -1)


def kernel(x, norm_ffn1, ffn1_w1, ffn1_w3, ffn1_w2, norm_mix, w_in, swa_sinks, rel_bias,
           w_branch_swa, w_branch_sb, w_out, norm_ffn2, ffn2_w1, ffn2_w3, ffn2_w2, norm_final):
    batch, seq, d = x.shape
    depth = norm_ffn1.shape[0]
    qa_w = SWA_Q_HEADS * HEAD_DIM
    kva_w = SWA_KV_HEADS * HEAD_DIM
    sb_w = SB_HEADS * HEAD_DIM
    scale = HEAD_DIM ** -0.5

    bias, band = _swa_bias(rel_bias), _band_mask()
    tri = (jnp.arange(SB_TK)[:, None] >= jnp.arange(SB_TK)[None, :]).astype(BF16)
    u = jnp.concatenate([tri, tri], axis=0)
    gain_final = norm_final.reshape(1, d)
    kva_dup = SWA_KV_HEADS * LANES
    c_qa, c_ka, c_va = 0, qa_w, qa_w + kva_dup
    c_qb = qa_w + 2 * kva_dup
    c_kb, c_vb = c_qb + sb_w, c_qb + 2 * sb_w

    xt = x.reshape(batch * seq, d)
    for layer in range(depth):
        w = w_in[layer]
        o = 0
        cols = {}
        for name, width in (("qa", qa_w), ("ka", kva_w), ("va", kva_w), ("qb", sb_w), ("kb", sb_w),
                            ("vb", sb_w), ("g", 2 * d)):
            cols[name] = w[:, o:o + width]
            o += width
        wqkv = jnp.concatenate([cols["qa"] * scale, _dup_heads(cols["ka"]), _dup_heads(cols["va"]),
                                cols["qb"] * -scale, cols["kb"], cols["vb"]], axis=1).astype(BF16)
        x1, qkv, gate = _pre(xt, norm_ffn1[layer].reshape(1, d), ffn1_w1[layer].astype(BF16),
                             ffn1_w3[layer].astype(BF16), ffn1_w2[layer].astype(BF16),
                             norm_mix[layer].reshape(1, d), wqkv, cols["g"].astype(BF16))
        oa = _swa(qkv, swa_sinks[layer], bias, band, batch=batch, seq=seq,
                  q_col=c_qa, k_col=c_ka, v_col=c_va)
        ob = _sb(qkv, u, batch=batch, seq=seq, q_col=c_qb, k_col=c_kb, v_col=c_vb)
        xt = _post(x1, oa, ob, gate, w_branch_swa[layer].astype(BF16),
                   w_branch_sb[layer].astype(BF16), w_out[layer].astype(BF16),
                   norm_ffn2[layer].reshape(1, d), ffn2_w1[layer].astype(BF16),
                   ffn2_w3[layer].astype(BF16), ffn2_w2[layer].astype(BF16),
                   gain_final, final_norm=layer == depth - 1)
    return xt.reshape(batch, seq, d)
```

```python
import functools
import math

import jax
import jax.numpy as jnp
import numpy as np
from jax import lax
from jax.experimental import pallas as pl
from jax.experimental.pallas import tpu as pltpu

F32 = jnp.float32
BF16 = jnp.bfloat16

HEAD_DIM = 64
SWA_Q_HEADS = 8
SWA_KV_HEADS = 2
SWA_WINDOW = 128
SB_HEADS = 8
BLOCK = 128
REL_BUCKETS = 32
REL_MAX_DIST = 128
RMS_EPS = 1e-6
NEG_BIG = -1e30

LANES = 128
HEADS_PER_TILE = LANES // HEAD_DIM
FFN_CHUNK = 256
PROJ_CHUNK = 256
TOKEN_TILE = 512
SB_TQ = 128
SB_UNIT = 128
SB_WINDOW_UNITS = 3
SB_GROUP = 8
LOG2E = math.log2(math.e)
SB_LOG2_UNDERFLOW = -150.0
SWA_STEP = 1024
VMEM_LIMIT = 60000 * 1024


def _rms(x, g):
    ms = jnp.mean(x * x, axis=-1, keepdims=True)
    return x * lax.rsqrt(ms + RMS_EPS) * g


def _dot(a, b):
    return jnp.dot(a, b, preferred_element_type=F32)


def _dot_nt(a, b):
    return lax.dot_general(a, b, (((1,), (1,)), ((), ())), preferred_element_type=F32)


def _resident(shape):
    nd = len(shape)
    return pl.BlockSpec(shape, lambda *_: (0,) * nd, pipeline_mode=pl.Buffered(1))


def _swiglu_into(acc_ref, h, w1_ref, w3_ref, w2_ref):
    d_ff = w1_ref.shape[1]
    for c in range(d_ff // FFN_CHUNK):
        cols = slice(c * FFN_CHUNK, (c + 1) * FFN_CHUNK)
        a = _dot(h, w1_ref[:, cols])
        b = _dot(h, w3_ref[:, cols])
        gated = (a * jax.nn.sigmoid(a) * b).astype(BF16)
        down = _dot(gated, w2_ref[cols, :])
        if c == 0:
            acc_ref[...] = down
        else:
            acc_ref[...] += down


def _pre_kernel(x_ref, g1_ref, w1_ref, w3_ref, w2_ref, gm_ref, wqkv_ref, wg_ref,
                x1_ref, qkv_ref, gate_ref, acc_ref):
    x = x_ref[...]
    _swiglu_into(acc_ref, _rms(x, g1_ref[...]).astype(BF16), w1_ref, w3_ref, w2_ref)
    x1 = x + 0.5 * acc_ref[...]
    x1_ref[...] = x1
    h = _rms(x1, gm_ref[...]).astype(BF16)
    for c in range(qkv_ref.shape[1] // PROJ_CHUNK):
        cols = slice(c * PROJ_CHUNK, (c + 1) * PROJ_CHUNK)
        qkv_ref[:, cols] = _dot(h, wqkv_ref[:, cols]).astype(BF16)
    for c in range(gate_ref.shape[1] // PROJ_CHUNK):
        cols = slice(c * PROJ_CHUNK, (c + 1) * PROJ_CHUNK)
        gate_ref[:, cols] = jax.nn.sigmoid(_dot(h, wg_ref[:, cols])).astype(BF16)


def _pre(x, g1, w1, w3, w2, gm, wqkv, wg):
    t, d = x.shape
    tm = TOKEN_TILE
    row = lambda i: (i, 0)
    return pl.pallas_call(
        _pre_kernel,
        grid=(t // tm,),
        in_specs=[pl.BlockSpec((tm, d), row), _resident(g1.shape), _resident(w1.shape),
                  _resident(w3.shape), _resident(w2.shape), _resident(gm.shape),
                  _resident(wqkv.shape), _resident(wg.shape)],
        out_specs=[pl.BlockSpec((tm, d), row), pl.BlockSpec((tm, wqkv.shape[1]), row),
                   pl.BlockSpec((tm, wg.shape[1]), row)],
        out_shape=[jax.ShapeDtypeStruct((t, d), F32),
                   jax.ShapeDtypeStruct((t, wqkv.shape[1]), BF16),
                   jax.ShapeDtypeStruct((t, wg.shape[1]), BF16)],
        scratch_shapes=[pltpu.VMEM((tm, d), F32)],
        compiler_params=pltpu.CompilerParams(
            dimension_semantics=("arbitrary",), vmem_limit_bytes=VMEM_LIMIT),
        name="pre",
    )(x, g1, w1, w3, w2, gm, wqkv, wg)


def _post_kernel(x_ref, oa_ref, ob_ref, gate_ref, wa_ref, wb_ref, wo_ref, g2_ref,
                 w1_ref, w3_ref, w2_ref, gf_ref, o_ref, acc_ref, mrg_ref, *, final_norm):
    d = x_ref.shape[1]
    oa = oa_ref[...]
    ob = ob_ref[...]
    for c in range(d // PROJ_CHUNK):
        cols = slice(c * PROJ_CHUNK, (c + 1) * PROJ_CHUNK)
        gcols = slice(d + c * PROJ_CHUNK, d + (c + 1) * PROJ_CHUNK)
        merged = (gate_ref[:, cols].astype(F32) * _dot(oa, wa_ref[:, cols])
                  + gate_ref[:, gcols].astype(F32) * _dot(ob, wb_ref[:, cols]))
        mrg_ref[:, cols] = merged.astype(BF16)
    x2 = x_ref[...] + _dot(mrg_ref[...], wo_ref[...])
    _swiglu_into(acc_ref, _rms(x2, g2_ref[...]).astype(BF16), w1_ref, w3_ref, w2_ref)
    y = x2 + 0.5 * acc_ref[...]
    if final_norm:
        y = _rms(y, gf_ref[...])
    o_ref[...] = y


def _post(x, oa, ob, gate, wa, wb, wo, g2, w1, w3, w2, gf, *, final_norm):
    t, d = x.shape
    tm = TOKEN_TILE
    row = lambda i: (i, 0)
    return pl.pallas_call(
        functools.partial(_post_kernel, final_norm=final_norm),
        grid=(t // tm,),
        in_specs=[pl.BlockSpec((tm, d), row), pl.BlockSpec((tm, oa.shape[1]), row),
                  pl.BlockSpec((tm, ob.shape[1]), row), pl.BlockSpec((tm, gate.shape[1]), row),
                  _resident(wa.shape), _resident(wb.shape), _resident(wo.shape),
                  _resident(g2.shape), _resident(w1.shape), _resident(w3.shape),
                  _resident(w2.shape), _resident(gf.shape)],
        out_specs=pl.BlockSpec((tm, d), row),
        out_shape=jax.ShapeDtypeStruct((t, d), F32),
        scratch_shapes=[pltpu.VMEM((tm, d), F32), pltpu.VMEM((tm, d), BF16)],
        compiler_params=pltpu.CompilerParams(
            dimension_semantics=("arbitrary",), vmem_limit_bytes=VMEM_LIMIT),
        name="post",
    )(x, oa, ob, gate, wa, wb, wo, g2, w1, w3, w2, gf)


def _swa_kernel(sink_ref, q_ref, kprev_ref, k_ref, vprev_ref, v_ref, bias_ref, band_ref,
                o_ref, kwin_ref, vwin_ref):
    j = pl.program_id(1)
    ts = q_ref.shape[0]
    kwin_ref[0:BLOCK, :] = kprev_ref[...]
    kwin_ref[BLOCK:, :] = k_ref[...]
    vwin_ref[0:BLOCK, :] = vprev_ref[...]
    vwin_ref[BLOCK:, :] = v_ref[...]
    lane = lax.broadcasted_iota(jnp.int32, (1, LANES), 1)
    low_half = lane < HEAD_DIM
    key_col = lax.broadcasted_iota(jnp.int32, (1, 2 * BLOCK), 1)
    band = band_ref[...] > 0.0
    group = SWA_Q_HEADS // SWA_KV_HEADS
    heads = range(SWA_Q_HEADS)
    sels = (low_half, jnp.logical_not(low_half))
    fills = [jnp.where(key_col == 0, sink_ref[hd], NEG_BIG) for hd in heads]
    not_row0 = lax.broadcasted_iota(jnp.int32, (2 * BLOCK, 1), 0) != 0
    vsels = tuple(jnp.logical_and(not_row0, s) for s in sels)

    def block(n, carry):
        r0 = pl.multiple_of(n * BLOCK, BLOCK)
        first = jnp.logical_and(j == 0, n == 0)
        valid = jnp.logical_and(band, jnp.logical_or(key_col >= BLOCK, jnp.logical_not(first)))
        kws = [kwin_ref[pl.ds(r0, 2 * BLOCK), g * LANES:(g + 1) * LANES]
               for g in range(SWA_KV_HEADS)]
        vws = [vwin_ref[pl.ds(r0, 2 * BLOCK), g * LANES:(g + 1) * LANES]
               for g in range(SWA_KV_HEADS)]
        logits = []
        for hd in heads:
            p, e = divmod(hd, HEADS_PER_TILE)
            q = q_ref[pl.ds(r0, BLOCK), p * LANES:(p + 1) * LANES]
            logits.append(_dot_nt(jnp.where(sels[e], q, jnp.zeros_like(q)), kws[hd // group]))
        probs, inv = [], []
        for hd in heads:
            lg = jnp.where(valid, logits[hd] + bias_ref[hd], fills[hd])
            pr = jnp.exp(lg - jnp.max(lg, axis=-1, keepdims=True))
            inv.append(1.0 / jnp.sum(pr, axis=-1, keepdims=True))
            probs.append(pr.astype(BF16))
        for p in range(SWA_Q_HEADS // HEADS_PER_TILE):
            acc = jnp.zeros((BLOCK, LANES), F32)
            for e in range(HEADS_PER_TILE):
                hd = p * HEADS_PER_TILE + e
                vw = vws[hd // group]
                acc = acc + _dot(probs[hd], jnp.where(vsels[e], vw, jnp.zeros_like(vw)))
            o_ref[pl.ds(r0, BLOCK), p * LANES:(p + 1) * LANES] = (
                acc * jnp.where(low_half, inv[p * HEADS_PER_TILE], inv[p * HEADS_PER_TILE + 1])
            ).astype(o_ref.dtype)
        return carry

    lax.fori_loop(0, ts // BLOCK, block, 0, unroll=2)


def _swa(qkv, sinks, bias, band, *, batch, seq, q_col, k_col, v_col):
    t = qkv.shape[0]
    ts = min(SWA_STEP, seq)
    steps = seq // ts
    bps = ts // BLOCK
    nblk = seq // BLOCK
    qw = SWA_Q_HEADS * HEAD_DIM
    kvw = SWA_KV_HEADS * LANES

    def main(col_blocks):
        return lambda b, j: (b * steps + j, col_blocks)

    def prev(col_blocks):
        return lambda b, j: (b * nblk + jnp.maximum(j * bps - 1, 0), col_blocks)

    return pl.pallas_call(
        _swa_kernel,
        grid=(batch, steps),
        in_specs=[
            pl.BlockSpec(memory_space=pltpu.SMEM),
            pl.BlockSpec((ts, qw), main(q_col // qw)),
            pl.BlockSpec((BLOCK, kvw), prev(k_col // kvw)),
            pl.BlockSpec((ts, kvw), main(k_col // kvw)),
            pl.BlockSpec((BLOCK, kvw), prev(v_col // kvw)),
            pl.BlockSpec((ts, kvw), main(v_col // kvw)),
            _resident(bias.shape),
            _resident(band.shape),
        ],
        out_specs=pl.BlockSpec((ts, qw), lambda b, j: (b * steps + j, 0)),
        out_shape=jax.ShapeDtypeStruct((t, qw), BF16),
        scratch_shapes=[pltpu.VMEM((ts + BLOCK, kvw), BF16), pltpu.VMEM((ts + BLOCK, kvw), BF16)],
        compiler_params=pltpu.CompilerParams(
            dimension_semantics=("arbitrary", "arbitrary"), vmem_limit_bytes=VMEM_LIMIT),
        name="swa",
    )(sinks, qkv, qkv, qkv, qkv, qkv, bias, band)


def _sb_kernel(q_ref, k_ref, v_ref, u_ref, o_ref):
    seq = q_ref.shape[0]
    lane = lax.broadcasted_iota(jnp.int32, (1, LANES), 1)
    low_half = lane < HEAD_DIM
    rows = lax.broadcasted_iota(jnp.int32, (SB_TQ, SB_UNIT), 0)
    cols = lax.broadcasted_iota(jnp.int32, (SB_TQ, SB_UNIT), 1)
    causal = jnp.concatenate([cols < rows] * HEADS_PER_TILE, axis=0)
    u = u_ref[...]

    def unit(x, i):
        return x[:, i * SB_UNIT:(i + 1) * SB_UNIT]

    def sweep(items):
        w2s, lhs = [], []
        for q2, start, n_u, diag, _, _ in items:
            keys = k_ref[pl.ds(start, n_u * SB_UNIT), :]
            w2 = _dot_nt(q2, keys) * LOG2E
            if diag:
                last = jnp.where(causal, unit(w2, n_u - 1), -NEG_BIG)
                w2 = jnp.concatenate([unit(w2, i) for i in range(n_u - 1)] + [last], axis=1)
            w2s.append(w2)
        for (_, _, n_u, _, _, _), w2 in zip(items, w2s):
            neg_abs = lax.bitcast_convert_type(
                lax.bitcast_convert_type(w2, jnp.uint32) | jnp.uint32(0x80000000), F32)
            log_keep = jnp.minimum(w2, 0.0) - jnp.log2(1.0 + jnp.exp2(neg_abs))
            hi = log_keep.astype(BF16)
            lo = (log_keep - hi.astype(F32)).astype(BF16)
            lhs += [jnp.concatenate([unit(hi, i), unit(lo, i)], axis=1) for i in range(n_u)]
        sums = _dot(jnp.concatenate(lhs, axis=0), u)
        weights, runnings = [], []
        row = 0
        for (_, _, n_u, _, running, _), w2 in zip(items, w2s):
            a = [None] * n_u
            for i in reversed(range(n_u)):
                part = sums[row + i * HEADS_PER_TILE * SB_TQ:row + (i + 1) * HEADS_PER_TILE * SB_TQ]
                local, total = unit(part, 0), unit(part, 1)
                c = local if running is None else local + running
                running = total if running is None else running + total
                a[i] = jnp.exp2(c - unit(w2, i)).astype(BF16)
            row += n_u * HEADS_PER_TILE * SB_TQ
            weights.append(jnp.concatenate(a, axis=1))
            runnings.append(running)
        out = []
        for (_, start, n_u, _, _, acc), a, running in zip(items, weights, runnings):
            both = _dot(a, v_ref[pl.ds(start, n_u * SB_UNIT), :])
            mine = jnp.where(low_half, both[:SB_TQ], both[SB_TQ:])
            out.append((running, mine if acc is None else acc + mine))
        return out

    def stacked_q(i):
        q = q_ref[pl.ds(pl.multiple_of(i * SB_TQ, SB_TQ), SB_TQ), :]
        return jnp.concatenate([jnp.where(low_half, q, jnp.zeros_like(q)),
                                jnp.where(low_half, jnp.zeros_like(q), q)], axis=0)

    def alive(r):
        return jnp.max(r) > SB_LOG2_UNDERFLOW

    def finish(i, q2, running, acc):
        def cond(state):
            return jnp.logical_and(state[0] >= 0, state[1])

        def body(state):
            ku, _, r, acc_ = state
            (r, acc_), = sweep([(q2, pl.multiple_of(ku * SB_UNIT, SB_UNIT), 1, False, r, acc_)])
            return ku - 1, alive(r), r, acc_

        return lax.while_loop(cond, body, (i - SB_WINDOW_UNITS, alive(running), running, acc))[3]

    def qblocks(idx, n_u):
        items = []
        for i in idx:
            first = pl.multiple_of((i - (n_u - 1)) * SB_UNIT, SB_UNIT)
            items.append((stacked_q(i), first, n_u, True, None, None))
        swept = sweep(items)
        accs = [acc for _, acc in swept]
        if n_u == SB_WINDOW_UNITS:
            worst = functools.reduce(jnp.maximum, [running for running, _ in swept])
            accs = lax.cond(
                alive(worst),
                lambda: [finish(i, item[0], running, acc)
                         for i, item, (running, acc) in zip(idx, items, swept)],
                lambda: accs)
        for i, acc in zip(idx, accs):
            o_ref[pl.ds(pl.multiple_of(i * SB_TQ, SB_TQ), SB_TQ), :] = acc.astype(o_ref.dtype)

    n_q = seq // SB_TQ
    n_lead = SB_WINDOW_UNITS - 1
    for i in range(n_lead):
        qblocks([i], i + 1)

    def step(j, carry):
        i = n_lead + SB_GROUP * j
        qblocks([i + g for g in range(SB_GROUP)], SB_WINDOW_UNITS)
        return carry

    n_groups = (n_q - n_lead) // SB_GROUP
    lax.fori_loop(0, n_groups, step, 0)
    tail = list(range(n_lead + n_groups * SB_GROUP, n_q))
    if tail:
        qblocks(tail, SB_WINDOW_UNITS)


def _sb(qkv, u, *, batch, seq, q_col, k_col, v_col):
    t = qkv.shape[0]
    pairs = SB_HEADS // HEADS_PER_TILE

    def col(c0):
        return lambda b, p: (b, c0 // LANES + p)

    return pl.pallas_call(
        _sb_kernel,
        grid=(batch, pairs),
        in_specs=[
            pl.BlockSpec((seq, LANES), col(q_col)),
            pl.BlockSpec((seq, LANES), col(k_col)),
            pl.BlockSpec((seq, LANES), col(v_col)),
            _resident(u.shape),
        ],
        out_specs=pl.BlockSpec((seq, LANES), lambda b, p: (b, p)),
        out_shape=jax.ShapeDtypeStruct((t, SB_HEADS * HEAD_DIM), BF16),
        compiler_params=pltpu.CompilerParams(
            dimension_semantics=("arbitrary", "arbitrary"), vmem_limit_bytes=VMEM_LIMIT),
        name="sb",
    )(qkv, qkv, qkv, u)


def _rel_bucket(dist):
    max_exact = REL_BUCKETS // 2
    d = jnp.maximum(dist, 1).astype(F32)
    large = max_exact + (jnp.log(d / max_exact) / math.log(REL_MAX_DIST / max_exact)
                         * (REL_BUCKETS - max_exact)).astype(jnp.int32)
    large = jnp.minimum(large, REL_BUCKETS - 1)
    return jnp.where(dist < max_exact, dist, large)


def _swa_bias(rel_table):
    f = rel_table.astype(F32)[_rel_bucket(jnp.arange(SWA_WINDOW))].T
    heads = f.shape[0]
    span = 3 * BLOCK
    v = jnp.pad(f, ((0, 0), (BLOCK - 1, span - SWA_WINDOW - (BLOCK - 1))))
    shifted = jnp.tile(v, (1, BLOCK + 1))[:, :BLOCK * (span + 1)].reshape(heads, BLOCK, span + 1)
    return shifted[:, :, :2 * BLOCK][:, :, ::-1]


def _band_mask():
    qi = np.arange(BLOCK)[:, None] + BLOCK
    kj = np.arange(2 * BLOCK)[None, :]
    dist = qi - kj
    return jnp.asarray(((dist >= 0) & (dist < SWA_WINDOW)).astype(np.float32))


def _dup_heads(w):
    d_in = w.shape[0]
    w = w.reshape(d_in, SWA_KV_HEADS, 1, HEAD_DIM)
    return jnp.broadcast_to(w, (d_in, SWA_KV_HEADS, HEADS_PER_TILE, HEAD_DIM)).reshape(d_in, -1)


def kernel(x, norm_ffn1, ffn1_w1, ffn1_w3, ffn1_w2, norm_mix, w_in, swa_sinks, rel_bias,
           w_branch_swa, w_branch_sb, w_out, norm_ffn2, ffn2_w1, ffn2_w3, ffn2_w2, norm_final):
    batch, seq, d = x.shape
    depth = norm_ffn1.shape[0]
    qa_w = SWA_Q_HEADS * HEAD_DIM
    kva_w = SWA_KV_HEADS * HEAD_DIM
    sb_w = SB_HEADS * HEAD_DIM
    scale = HEAD_DIM ** -0.5

    bias, band = _swa_bias(rel_bias), _band_mask()
    tri = (jnp.arange(SB_UNIT)[:, None] >= jnp.arange(SB_UNIT)[None, :]).astype(BF16)
    half = jnp.concatenate([tri, jnp.ones_like(tri)], axis=1)
    u = jnp.concatenate([half, half], axis=0)
    gain_final = norm_final.reshape(1, d)
    kva_dup = SWA_KV_HEADS * LANES
    c_qa, c_ka, c_va = 0, qa_w, qa_w + kva_dup
    c_qb = qa_w + 2 * kva_dup
    c_kb, c_vb = c_qb + sb_w, c_qb + 2 * sb_w

    xt = x.reshape(batch * seq, d)
    for layer in range(depth):
        w = w_in[layer]
        o = 0
        cols = {}
        for name, width in (("qa", qa_w), ("ka", kva_w), ("va", kva_w), ("qb", sb_w), ("kb", sb_w),
                            ("vb", sb_w), ("g", 2 * d)):
            cols[name] = w[:, o:o + width]
            o += width
        wqkv = jnp.concatenate([cols["qa"] * scale, _dup_heads(cols["ka"]), _dup_heads(cols["va"]),
                                cols["qb"] * -scale, cols["kb"], cols["vb"]], axis=1).astype(BF16)
        x1, qkv, gate = _pre(xt, norm_ffn1[layer].reshape(1, d), ffn1_w1[layer].astype(BF16),
                             ffn1_w3[layer].astype(BF16), ffn1_w2[layer].astype(BF16),
                             norm_mix[layer].reshape(1, d), wqkv, cols["g"].astype(BF16))
        oa = _swa(qkv, swa_sinks[layer], bias, band, batch=batch, seq=seq,
                  q_col=c_qa, k_col=c_ka, v_col=c_va)
        ob = _sb(qkv, u, batch=batch, seq=seq, q_col=c_qb, k_col=c_kb, v_col=c_vb)
        xt = _post(x1, oa, ob, gate, w_branch_swa[layer].astype(BF16),
                   w_branch_sb[layer].astype(BF16), w_out[layer].astype(BF16),
                   norm_ffn2[layer].reshape(1, d), ffn2_w1[layer].astype(BF16),
                   ffn2_w3[layer].astype(BF16), ffn2_w2[layer].astype(BF16),
                   gain_final, final_norm=layer == depth - 1)
    return xt.reshape(batch, seq, d)
```

```python
import functools
import math

import jax
import jax.numpy as jnp
import numpy as np
from jax import lax
from jax.experimental import pallas as pl
from jax.experimental.pallas import tpu as pltpu

F32 = jnp.float32
BF16 = jnp.bfloat16

HEAD_DIM = 64
SWA_Q_HEADS = 8
SWA_KV_HEADS = 2
SWA_WINDOW = 128
SB_HEADS = 8
BLOCK = 128
REL_BUCKETS = 32
REL_MAX_DIST = 128
RMS_EPS = 1e-6
NEG_BIG = -1e30

LANES = 128
HEADS_PER_TILE = LANES // HEAD_DIM
FFN_CHUNK = 256
PROJ_CHUNK = 256
SWA_FIRST_CHUNK = 2
SWA_CHUNK_STRIDE = 2
TOKEN_TILE = 512
SB_TQ = 128
SB_UNIT = 128
SB_WINDOW_UNITS = 3
SB_GROUP = 8
LOG2E = math.log2(math.e)
SB_LOG2_UNDERFLOW = -150.0
VMEM_LIMIT = 60000 * 1024


def _rms(x, g):
    ms = jnp.mean(x * x, axis=-1, keepdims=True)
    return x * lax.rsqrt(ms + RMS_EPS) * g


def _dot(a, b):
    return jnp.dot(a, b, preferred_element_type=F32)


def _dot_nt(a, b):
    return lax.dot_general(a, b, (((1,), (1,)), ((), ())), preferred_element_type=F32)


def _resident(shape):
    nd = len(shape)
    return pl.BlockSpec(shape, lambda *_: (0,) * nd, pipeline_mode=pl.Buffered(1))


def _swiglu_into(acc_ref, h, w1_ref, w3_ref, w2_ref, before_chunk=None):
    d_ff = w1_ref.shape[1]
    for c in range(d_ff // FFN_CHUNK):
        if before_chunk and c in before_chunk:
            before_chunk[c]()
        cols = slice(c * FFN_CHUNK, (c + 1) * FFN_CHUNK)
        a = _dot(h, w1_ref[:, cols])
        b = _dot(h, w3_ref[:, cols])
        gated = (a * jax.nn.sigmoid(a) * b).astype(BF16)
        down = _dot(gated, w2_ref[cols, :])
        if c == 0:
            acc_ref[...] = down
        else:
            acc_ref[...] += down


def _pre_kernel(x_ref, g1_ref, w1_ref, w3_ref, w2_ref, gm_ref, wqkv_ref, wg_ref,
                x1_ref, qkv_ref, gate_ref, acc_ref):
    x = x_ref[...]
    _swiglu_into(acc_ref, _rms(x, g1_ref[...]).astype(BF16), w1_ref, w3_ref, w2_ref)
    x1 = x + 0.5 * acc_ref[...]
    x1_ref[...] = x1
    h = _rms(x1, gm_ref[...]).astype(BF16)
    for c in range(qkv_ref.shape[1] // PROJ_CHUNK):
        cols = slice(c * PROJ_CHUNK, (c + 1) * PROJ_CHUNK)
        qkv_ref[:, cols] = _dot(h, wqkv_ref[:, cols]).astype(BF16)
    for c in range(gate_ref.shape[1] // PROJ_CHUNK):
        cols = slice(c * PROJ_CHUNK, (c + 1) * PROJ_CHUNK)
        gate_ref[:, cols] = jax.nn.sigmoid(_dot(h, wg_ref[:, cols])).astype(BF16)


def _pre(x, g1, w1, w3, w2, gm, wqkv, wg):
    t, d = x.shape
    tm = TOKEN_TILE
    row = lambda i: (i, 0)
    return pl.pallas_call(
        _pre_kernel,
        grid=(t // tm,),
        in_specs=[pl.BlockSpec((tm, d), row), _resident(g1.shape), _resident(w1.shape),
                  _resident(w3.shape), _resident(w2.shape), _resident(gm.shape),
                  _resident(wqkv.shape), _resident(wg.shape)],
        out_specs=[pl.BlockSpec((tm, d), row), pl.BlockSpec((tm, wqkv.shape[1]), row),
                   pl.BlockSpec((tm, wg.shape[1]), row)],
        out_shape=[jax.ShapeDtypeStruct((t, d), F32),
                   jax.ShapeDtypeStruct((t, wqkv.shape[1]), BF16),
                   jax.ShapeDtypeStruct((t, wg.shape[1]), BF16)],
        scratch_shapes=[pltpu.VMEM((tm, d), F32)],
        compiler_params=pltpu.CompilerParams(
            dimension_semantics=("arbitrary",), vmem_limit_bytes=VMEM_LIMIT),
        name="pre",
    )(x, g1, w1, w3, w2, gm, wqkv, wg)


def _post_kernel(sink_ref, x_ref, ob_ref, gate_ref, q_ref, kprev_ref, k_ref, vprev_ref, v_ref,
                 bias_ref, band_ref, wa_ref, wb_ref, wo_ref, g2_ref, w1_ref, w3_ref, w2_ref, gf_ref,
                 o_ref, acc_ref, mrg_ref, oa_ref, kwin_ref, vwin_ref, *, final_norm, seq):
    s = pl.program_id(0)
    tm, d = x_ref.shape
    tile = jnp.minimum(s, pl.num_programs(0) - 2)
    first_rows = lax.rem(tile * tm, seq) == 0

    @pl.when(s == 0)
    def _():
        oa_ref[...] = jnp.zeros_like(oa_ref)

    attend = _swa_stages(sink_ref, q_ref, kprev_ref, k_ref, vprev_ref, v_ref, bias_ref, band_ref,
                         kwin_ref, vwin_ref, first_rows)
    logits = attend.logits()
    oa = oa_ref[...]
    ob = ob_ref[...]
    for c in range(d // PROJ_CHUNK):
        cols = slice(c * PROJ_CHUNK, (c + 1) * PROJ_CHUNK)
        gcols = slice(d + c * PROJ_CHUNK, d + (c + 1) * PROJ_CHUNK)
        merged = (gate_ref[:, cols].astype(F32) * _dot(oa, wa_ref[:, cols])
                  + gate_ref[:, gcols].astype(F32) * _dot(ob, wb_ref[:, cols]))
        mrg_ref[:, cols] = merged.astype(BF16)
    x2 = x_ref[...] + _dot(mrg_ref[...], wo_ref[...])
    spread = {SWA_FIRST_CHUNK + SWA_CHUNK_STRIDE * n:
              functools.partial(attend.finish_block, n, logits, oa_ref)
              for n in range(attend.blocks)}
    _swiglu_into(acc_ref, _rms(x2, g2_ref[...]).astype(BF16), w1_ref, w3_ref, w2_ref,
                 before_chunk=spread)
    y = x2 + 0.5 * acc_ref[...]
    if final_norm:
        y = _rms(y, gf_ref[...])
    o_ref[...] = y


def _post(x, ob, gate, qkv, sinks, bias, band, wa, wb, wo, g2, w1, w3, w2, gf, *, final_norm, seq,
          q_col, k_col, v_col):
    t, d = x.shape
    tm = TOKEN_TILE
    n = t // tm
    qw = SWA_Q_HEADS * HEAD_DIM
    kvw = SWA_KV_HEADS * LANES
    blocks_per_tile = tm // BLOCK
    done = lambda s: (jnp.maximum(s - 1, 0), 0)

    def ahead(col_block):
        return lambda s: (jnp.minimum(s, n - 1), col_block)

    def before(col_block):
        return lambda s: (jnp.maximum(jnp.minimum(s, n - 1) * blocks_per_tile - 1, 0), col_block)

    return pl.pallas_call(
        functools.partial(_post_kernel, final_norm=final_norm, seq=seq),
        grid=(n + 1,),
        in_specs=[pl.BlockSpec(memory_space=pltpu.SMEM),
                  pl.BlockSpec((tm, d), done), pl.BlockSpec((tm, ob.shape[1]), done),
                  pl.BlockSpec((tm, gate.shape[1]), done),
                  pl.BlockSpec((tm, qw), ahead(q_col // qw)),
                  pl.BlockSpec((BLOCK, kvw), before(k_col // kvw)),
                  pl.BlockSpec((tm, kvw), ahead(k_col // kvw)),
                  pl.BlockSpec((BLOCK, kvw), before(v_col // kvw)),
                  pl.BlockSpec((tm, kvw), ahead(v_col // kvw)),
                  _resident(bias.shape), _resident(band.shape),
                  _resident(wa.shape), _resident(wb.shape), _resident(wo.shape),
                  _resident(g2.shape), _resident(w1.shape), _resident(w3.shape),
                  _resident(w2.shape), _resident(gf.shape)],
        out_specs=pl.BlockSpec((tm, d), done),
        out_shape=jax.ShapeDtypeStruct((t, d), F32),
        scratch_shapes=[pltpu.VMEM((tm, d), F32), pltpu.VMEM((tm, d), BF16),
                        pltpu.VMEM((tm, qw), BF16),
                        pltpu.VMEM((tm + BLOCK, kvw), BF16), pltpu.VMEM((tm + BLOCK, kvw), BF16)],
        compiler_params=pltpu.CompilerParams(
            dimension_semantics=("arbitrary",), vmem_limit_bytes=VMEM_LIMIT),
        name="post",
    )(sinks, x, ob, gate, qkv, qkv, qkv, qkv, qkv, bias, band, wa, wb, wo, g2, w1, w3, w2, gf)


class _swa_stages:
    def __init__(self, sink_ref, q_ref, kprev_ref, k_ref, vprev_ref, v_ref, bias_ref, band_ref,
                 kwin_ref, vwin_ref, first_rows):
        kwin_ref[0:BLOCK, :] = kprev_ref[...]
        kwin_ref[BLOCK:, :] = k_ref[...]
        vwin_ref[0:BLOCK, :] = vprev_ref[...]
        vwin_ref[BLOCK:, :] = v_ref[...]
        self.q_ref, self.kwin_ref, self.vwin_ref, self.bias_ref = q_ref, kwin_ref, vwin_ref, bias_ref
        self.blocks = q_ref.shape[0] // BLOCK
        lane = lax.broadcasted_iota(jnp.int32, (1, LANES), 1)
        self.low_half = lane < HEAD_DIM
        self.sels = (self.low_half, jnp.logical_not(self.low_half))
        key_col = lax.broadcasted_iota(jnp.int32, (1, 2 * BLOCK), 1)
        band = band_ref[...] > 0.0
        self.valid = [band] * self.blocks
        self.valid[0] = jnp.logical_and(
            band, jnp.logical_or(key_col >= BLOCK, jnp.logical_not(first_rows)))
        self.fills = [jnp.where(key_col == 0, sink_ref[hd], NEG_BIG) for hd in range(SWA_Q_HEADS)]
        not_row0 = lax.broadcasted_iota(jnp.int32, (2 * BLOCK, 1), 0) != 0
        self.vsels = tuple(jnp.logical_and(not_row0, sel) for sel in self.sels)

    def logits(self):
        group = SWA_Q_HEADS // SWA_KV_HEADS
        out = []
        for n in range(self.blocks):
            kws = [self.kwin_ref[n * BLOCK:(n + 2) * BLOCK, g * LANES:(g + 1) * LANES]
                   for g in range(SWA_KV_HEADS)]
            for hd in range(SWA_Q_HEADS):
                p, e = divmod(hd, HEADS_PER_TILE)
                q = self.q_ref[n * BLOCK:(n + 1) * BLOCK, p * LANES:(p + 1) * LANES]
                out.append(_dot_nt(jnp.where(self.sels[e], q, jnp.zeros_like(q)), kws[hd // group]))
        return out

    def finish_block(self, n, logits, o_ref):
        group = SWA_Q_HEADS // SWA_KV_HEADS
        probs = []
        for hd in range(SWA_Q_HEADS):
            lg = jnp.where(self.valid[n], logits[n * SWA_Q_HEADS + hd] + self.bias_ref[hd],
                           self.fills[hd])
            pr = jnp.exp(lg - jnp.max(lg, axis=-1, keepdims=True))
            probs.append((pr.astype(BF16), 1.0 / jnp.sum(pr, axis=-1, keepdims=True)))
        vws = [self.vwin_ref[n * BLOCK:(n + 2) * BLOCK, g * LANES:(g + 1) * LANES]
               for g in range(SWA_KV_HEADS)]
        for p in range(SWA_Q_HEADS // HEADS_PER_TILE):
            acc = jnp.zeros((BLOCK, LANES), F32)
            for e in range(HEADS_PER_TILE):
                hd = p * HEADS_PER_TILE + e
                vw = vws[hd // group]
                acc = acc + _dot(probs[hd][0], jnp.where(self.vsels[e], vw, jnp.zeros_like(vw)))
            inv = [probs[p * HEADS_PER_TILE + e][1] for e in range(HEADS_PER_TILE)]
            o_ref[n * BLOCK:(n + 1) * BLOCK, p * LANES:(p + 1) * LANES] = (
                acc * jnp.where(self.low_half, inv[0], inv[1])).astype(o_ref.dtype)


def _sb_kernel(q_ref, k_ref, v_ref, u_ref, o_ref):
    seq = q_ref.shape[0]
    lane = lax.broadcasted_iota(jnp.int32, (1, LANES), 1)
    low_half = lane < HEAD_DIM
    rows = lax.broadcasted_iota(jnp.int32, (SB_TQ, SB_UNIT), 0)
    cols = lax.broadcasted_iota(jnp.int32, (SB_TQ, SB_UNIT), 1)
    causal = jnp.concatenate([cols < rows] * HEADS_PER_TILE, axis=0)
    u = u_ref[...]

    def unit(x, i):
        return x[:, i * SB_UNIT:(i + 1) * SB_UNIT]

    def sweep(items):
        w2s, lhs = [], []
        for q2, start, n_u, diag, _, _ in items:
            keys = k_ref[pl.ds(start, n_u * SB_UNIT), :]
            w2 = _dot_nt(q2, keys) * LOG2E
            if diag:
                last = jnp.where(causal, unit(w2, n_u - 1), -NEG_BIG)
                w2 = jnp.concatenate([unit(w2, i) for i in range(n_u - 1)] + [last], axis=1)
            w2s.append(w2)
        for (_, _, n_u, _, _, _), w2 in zip(items, w2s):
            neg_abs = lax.bitcast_convert_type(
                lax.bitcast_convert_type(w2, jnp.uint32) | jnp.uint32(0x80000000), F32)
            log_keep = jnp.minimum(w2, 0.0) - jnp.log2(1.0 + jnp.exp2(neg_abs))
            hi = log_keep.astype(BF16)
            lo = (log_keep - hi.astype(F32)).astype(BF16)
            lhs += [jnp.concatenate([unit(hi, i), unit(lo, i)], axis=1) for i in range(n_u)]
        sums = _dot(jnp.concatenate(lhs, axis=0), u)
        weights, runnings = [], []
        row = 0
        for (_, _, n_u, _, running, _), w2 in zip(items, w2s):
            a = [None] * n_u
            for i in reversed(range(n_u)):
                part = sums[row + i * HEADS_PER_TILE * SB_TQ:row + (i + 1) * HEADS_PER_TILE * SB_TQ]
                local, total = unit(part, 0), unit(part, 1)
                c = local if running is None else local + running
                running = total if running is None else running + total
                a[i] = jnp.exp2(c - unit(w2, i)).astype(BF16)
            row += n_u * HEADS_PER_TILE * SB_TQ
            weights.append(jnp.concatenate(a, axis=1))
            runnings.append(running)
        out = []
        for (_, start, n_u, _, _, acc), a, running in zip(items, weights, runnings):
            both = _dot(a, v_ref[pl.ds(start, n_u * SB_UNIT), :])
            mine = jnp.where(low_half, both[:SB_TQ], both[SB_TQ:])
            out.append((running, mine if acc is None else acc + mine))
        return out

    def stacked_q(i):
        q = q_ref[pl.ds(pl.multiple_of(i * SB_TQ, SB_TQ), SB_TQ), :]
        return jnp.concatenate([jnp.where(low_half, q, jnp.zeros_like(q)),
                                jnp.where(low_half, jnp.zeros_like(q), q)], axis=0)

    def alive(r):
        return jnp.max(r) > SB_LOG2_UNDERFLOW

    def finish(i, q2, running, acc):
        def cond(state):
            return jnp.logical_and(state[0] >= 0, state[1])

        def body(state):
            ku, _, r, acc_ = state
            (r, acc_), = sweep([(q2, pl.multiple_of(ku * SB_UNIT, SB_UNIT), 1, False, r, acc_)])
            return ku - 1, alive(r), r, acc_

        return lax.while_loop(cond, body, (i - SB_WINDOW_UNITS, alive(running), running, acc))[3]

    def qblocks(idx, n_u):
        items = []
        for i in idx:
            first = pl.multiple_of((i - (n_u - 1)) * SB_UNIT, SB_UNIT)
            items.append((stacked_q(i), first, n_u, True, None, None))
        swept = sweep(items)
        accs = [acc for _, acc in swept]
        if n_u == SB_WINDOW_UNITS:
            worst = functools.reduce(jnp.maximum, [running for running, _ in swept])
            accs = lax.cond(
                alive(worst),
                lambda: [finish(i, item[0], running, acc)
                         for i, item, (running, acc) in zip(idx, items, swept)],
                lambda: accs)
        for i, acc in zip(idx, accs):
            o_ref[pl.ds(pl.multiple_of(i * SB_TQ, SB_TQ), SB_TQ), :] = acc.astype(o_ref.dtype)

    n_q = seq // SB_TQ
    n_lead = SB_WINDOW_UNITS - 1
    for i in range(n_lead):
        qblocks([i], i + 1)

    def step(j, carry):
        i = n_lead + SB_GROUP * j
        qblocks([i + g for g in range(SB_GROUP)], SB_WINDOW_UNITS)
        return carry

    n_groups = (n_q - n_lead) // SB_GROUP
    lax.fori_loop(0, n_groups, step, 0)
    tail = list(range(n_lead + n_groups * SB_GROUP, n_q))
    if tail:
        qblocks(tail, SB_WINDOW_UNITS)


def _sb(qkv, u, *, batch, seq, q_col, k_col, v_col):
    t = qkv.shape[0]
    pairs = SB_HEADS // HEADS_PER_TILE

    def col(c0):
        return lambda b, p: (b, c0 // LANES + p)

    return pl.pallas_call(
        _sb_kernel,
        grid=(batch, pairs),
        in_specs=[
            pl.BlockSpec((seq, LANES), col(q_col)),
            pl.BlockSpec((seq, LANES), col(k_col)),
            pl.BlockSpec((seq, LANES), col(v_col)),
            _resident(u.shape),
        ],
        out_specs=pl.BlockSpec((seq, LANES), lambda b, p: (b, p)),
        out_shape=jax.ShapeDtypeStruct((t, SB_HEADS * HEAD_DIM), BF16),
        compiler_params=pltpu.CompilerParams(
            dimension_semantics=("arbitrary", "arbitrary"), vmem_limit_bytes=VMEM_LIMIT),
        name="sb",
    )(qkv, qkv, qkv, u)


def _rel_bucket(dist):
    max_exact = REL_BUCKETS // 2
    d = jnp.maximum(dist, 1).astype(F32)
    large = max_exact + (jnp.log(d / max_exact) / math.log(REL_MAX_DIST / max_exact)
                         * (REL_BUCKETS - max_exact)).astype(jnp.int32)
    large = jnp.minimum(large, REL_BUCKETS - 1)
    return jnp.where(dist < max_exact, dist, large)


def _swa_bias(rel_table):
    f = rel_table.astype(F32)[_rel_bucket(jnp.arange(SWA_WINDOW))].T
    heads = f.shape[0]
    span = 3 * BLOCK
    v = jnp.pad(f, ((0, 0), (BLOCK - 1, span - SWA_WINDOW - (BLOCK - 1))))
    shifted = jnp.tile(v, (1, BLOCK + 1))[:, :BLOCK * (span + 1)].reshape(heads, BLOCK, span + 1)
    return shifted[:, :, :2 * BLOCK][:, :, ::-1]


def _band_mask():
    qi = np.arange(BLOCK)[:, None] + BLOCK
    kj = np.arange(2 * BLOCK)[None, :]
    dist = qi - kj
    return jnp.asarray(((dist >= 0) & (dist < SWA_WINDOW)).astype(np.float32))


def _dup_heads(w):
    d_in = w.shape[0]
    w = w.reshape(d_in, SWA_KV_HEADS, 1, HEAD_DIM)
    return jnp.broadcast_to(w, (d_in, SWA_KV_HEADS, HEADS_PER_TILE, HEAD_DIM)).reshape(d_in, -1)


def kernel(x, norm_ffn1, ffn1_w1, ffn1_w3, ffn1_w2, norm_mix, w_in, swa_sinks, rel_bias,
           w_branch_swa, w_branch_sb, w_out, norm_ffn2, ffn2_w1, ffn2_w3, ffn2_w2, norm_final):
    batch, seq, d = x.shape
    depth = norm_ffn1.shape[0]
    qa_w = SWA_Q_HEADS * HEAD_DIM
    kva_w = SWA_KV_HEADS * HEAD_DIM
    sb_w = SB_HEADS * HEAD_DIM
    scale = HEAD_DIM ** -0.5

    bias, band = _swa_bias(rel_bias), _band_mask()
    tri = (jnp.arange(SB_UNIT)[:, None] >= jnp.arange(SB_UNIT)[None, :]).astype(BF16)
    half = jnp.concatenate([tri, jnp.ones_like(tri)], axis=1)
    u = jnp.concatenate([half, half], axis=0)
    gain_final = norm_final.reshape(1, d)
    kva_dup = SWA_KV_HEADS * LANES
    c_qa, c_ka, c_va = 0, qa_w, qa_w + kva_dup
    c_qb = qa_w + 2 * kva_dup
    c_kb, c_vb = c_qb + sb_w, c_qb + 2 * sb_w

    xt = x.reshape(batch * seq, d)
    for layer in range(depth):
        w = w_in[layer]
        o = 0
        cols = {}
        for name, width in (("qa", qa_w), ("ka", kva_w), ("va", kva_w), ("qb", sb_w), ("kb", sb_w),
                            ("vb", sb_w), ("g", 2 * d)):
            cols[name] = w[:, o:o + width]
            o += width
        wqkv = jnp.concatenate([cols["qa"] * scale, _dup_heads(cols["ka"]), _dup_heads(cols["va"]),
                                cols["qb"] * -scale, cols["kb"], cols["vb"]], axis=1).astype(BF16)
        x1, qkv, gate = _pre(xt, norm_ffn1[layer].reshape(1, d), ffn1_w1[layer].astype(BF16),
                             ffn1_w3[layer].astype(BF16), ffn1_w2[layer].astype(BF16),
                             norm_mix[layer].reshape(1, d), wqkv, cols["g"].astype(BF16))
        ob = _sb(qkv, u, batch=batch, seq=seq, q_col=c_qb, k_col=c_kb, v_col=c_vb)
        xt = _post(x1, ob, gate, qkv, swa_sinks[layer], bias, band,
                   w_branch_swa[layer].astype(BF16), w_branch_sb[layer].astype(BF16),
                   w_out[layer].astype(BF16), norm_ffn2[layer].reshape(1, d),
                   ffn2_w1[layer].astype(BF16), ffn2_w3[layer].astype(BF16),
                   ffn2_w2[layer].astype(BF16), gain_final, final_norm=layer == depth - 1,
                   seq=seq, q_col=c_qa, k_col=c_ka, v_col=c_va)
    return xt.reshape(batch, seq, d)
```

```python
import functools
import math

import jax
import jax.numpy as jnp
import numpy as np
from jax import lax
from jax.experimental import pallas as pl
from jax.experimental.pallas import tpu as pltpu

F32 = jnp.float32
BF16 = jnp.bfloat16

HEAD_DIM = 64
SWA_Q_HEADS = 8
SWA_KV_HEADS = 2
SWA_WINDOW = 128
SB_HEADS = 8
BLOCK = 128
REL_BUCKETS = 32
REL_MAX_DIST = 128
RMS_EPS = 1e-6
NEG_BIG = -1e30

LANES = 128
HEADS_PER_TILE = LANES // HEAD_DIM
FFN_CHUNK = 256
PROJ_CHUNK = 256
SWA_FIRST_CHUNK = 1
SWA_CHUNK_STRIDE = 2
SWA_PV_LAG = 2
TOKEN_TILE = 512
SB_TQ = 128
SB_UNIT = 128
SB_WINDOW_UNITS = 3
SB_GROUP = 16
LOG2E = math.log2(math.e)
SB_LOG2_UNDERFLOW = -150.0
VMEM_LIMIT = 60000 * 1024


def _rms(x, g):
    ms = jnp.mean(x * x, axis=-1, keepdims=True)
    return x * lax.rsqrt(ms + RMS_EPS) * g


def _dot(a, b):
    return jnp.dot(a, b, preferred_element_type=F32)


def _dot_nt(a, b):
    return lax.dot_general(a, b, (((1,), (1,)), ((), ())), preferred_element_type=F32)


def _resident(shape):
    nd = len(shape)
    return pl.BlockSpec(shape, lambda *_: (0,) * nd, pipeline_mode=pl.Buffered(1))


def _swiglu_into(acc_ref, h, w1_ref, w3_ref, w2_ref, before_chunk=None):
    d_ff = w1_ref.shape[1]
    for c in range(d_ff // FFN_CHUNK):
        for work in (before_chunk or {}).get(c, ()):
            work()
        cols = slice(c * FFN_CHUNK, (c + 1) * FFN_CHUNK)
        a = _dot(h, w1_ref[:, cols])
        b = _dot(h, w3_ref[:, cols])
        gated = (a * jax.nn.sigmoid(a) * b).astype(BF16)
        down = _dot(gated, w2_ref[cols, :])
        if c == 0:
            acc_ref[...] = down
        else:
            acc_ref[...] += down


def _pre_kernel(x_ref, g1_ref, w1_ref, w3_ref, w2_ref, gm_ref, wqkv_ref, wg_ref,
                x1_ref, qkv_ref, gate_ref, acc_ref):
    x = x_ref[...]
    _swiglu_into(acc_ref, _rms(x, g1_ref[...]).astype(BF16), w1_ref, w3_ref, w2_ref)
    x1 = x + 0.5 * acc_ref[...]
    x1_ref[...] = x1
    h = _rms(x1, gm_ref[...]).astype(BF16)
    n_qkv = qkv_ref.shape[1] // PROJ_CHUNK
    for c in range(n_qkv):
        cols = slice(c * PROJ_CHUNK, (c + 1) * PROJ_CHUNK)
        if c == n_qkv - 1 and n_qkv % 2:
            half = h.shape[0] // 2
            for rows in (slice(0, half), slice(half, None)):
                qkv_ref[rows, cols] = _dot(h[rows], wqkv_ref[:, cols]).astype(BF16)
        else:
            qkv_ref[:, cols] = _dot(h, wqkv_ref[:, cols]).astype(BF16)
    for c in range(gate_ref.shape[1] // PROJ_CHUNK):
        cols = slice(c * PROJ_CHUNK, (c + 1) * PROJ_CHUNK)
        gate_ref[:, cols] = jax.nn.sigmoid(_dot(h, wg_ref[:, cols])).astype(BF16)


def _pre(x, g1, w1, w3, w2, gm, wqkv, wg):
    t, d = x.shape
    tm = TOKEN_TILE
    row = lambda i: (i, 0)
    return pl.pallas_call(
        _pre_kernel,
        grid=(t // tm,),
        in_specs=[pl.BlockSpec((tm, d), row), _resident(g1.shape), _resident(w1.shape),
                  _resident(w3.shape), _resident(w2.shape), _resident(gm.shape),
                  _resident(wqkv.shape), _resident(wg.shape)],
        out_specs=[pl.BlockSpec((tm, d), row), pl.BlockSpec((tm, wqkv.shape[1]), row),
                   pl.BlockSpec((tm, wg.shape[1]), row)],
        out_shape=[jax.ShapeDtypeStruct((t, d), F32),
                   jax.ShapeDtypeStruct((t, wqkv.shape[1]), BF16),
                   jax.ShapeDtypeStruct((t, wg.shape[1]), BF16)],
        scratch_shapes=[pltpu.VMEM((tm, d), F32)],
        compiler_params=pltpu.CompilerParams(
            dimension_semantics=("arbitrary",), vmem_limit_bytes=VMEM_LIMIT),
        name="pre",
    )(x, g1, w1, w3, w2, gm, wqkv, wg)


def _post_kernel(sink_ref, x_ref, ob_ref, gate_ref, q_ref, kprev_ref, k_ref, vprev_ref, v_ref,
                 bias_ref, band_ref, wa_ref, wb_ref, wo_ref, g2_ref, w1_ref, w3_ref, w2_ref, gf_ref,
                 o_ref, acc_ref, mrg_ref, oa_ref, kwin_ref, vwin_ref, *, final_norm, seq):
    s = pl.program_id(0)
    tm, d = x_ref.shape
    tile = jnp.minimum(s, pl.num_programs(0) - 2)
    first_rows = lax.rem(tile * tm, seq) == 0

    @pl.when(s == 0)
    def _():
        oa_ref[...] = jnp.zeros_like(oa_ref)

    attend = _swa_stages(sink_ref, q_ref, kprev_ref, k_ref, vprev_ref, v_ref, bias_ref, band_ref,
                         kwin_ref, vwin_ref, first_rows)
    logits = attend.logits()
    oa = oa_ref[...]
    ob = ob_ref[...]
    for c in range(d // PROJ_CHUNK):
        cols = slice(c * PROJ_CHUNK, (c + 1) * PROJ_CHUNK)
        gcols = slice(d + c * PROJ_CHUNK, d + (c + 1) * PROJ_CHUNK)
        merged = (gate_ref[:, cols].astype(F32) * _dot(oa, wa_ref[:, cols])
                  + gate_ref[:, gcols].astype(F32) * _dot(ob, wb_ref[:, cols]))
        mrg_ref[:, cols] = merged.astype(BF16)
    x2 = x_ref[...] + _dot(mrg_ref[...], wo_ref[...])
    spread = {}
    for n in range(attend.blocks):
        first = SWA_FIRST_CHUNK + SWA_CHUNK_STRIDE * n
        spread.setdefault(first, []).append(functools.partial(attend.softmax_block, n, logits))
        spread.setdefault(first + SWA_PV_LAG, []).append(
            functools.partial(attend.pv_block, n, oa_ref))
    _swiglu_into(acc_ref, _rms(x2, g2_ref[...]).astype(BF16), w1_ref, w3_ref, w2_ref,
                 before_chunk=spread)
    y = x2 + 0.5 * acc_ref[...]
    if final_norm:
        y = _rms(y, gf_ref[...])
    o_ref[...] = y


def _post(x, ob, gate, qkv, sinks, bias, band, wa, wb, wo, g2, w1, w3, w2, gf, *, final_norm, seq,
          q_col, k_col, v_col):
    t, d = x.shape
    tm = TOKEN_TILE
    n = t // tm
    qw = SWA_Q_HEADS * HEAD_DIM
    kvw = SWA_KV_HEADS * HEAD_DIM
    blocks_per_tile = tm // BLOCK
    done = lambda s: (jnp.maximum(s - 1, 0), 0)

    def ahead(col_block):
        return lambda s: (jnp.minimum(s, n - 1), col_block)

    def before(col_block):
        return lambda s: (jnp.maximum(jnp.minimum(s, n - 1) * blocks_per_tile - 1, 0), col_block)

    return pl.pallas_call(
        functools.partial(_post_kernel, final_norm=final_norm, seq=seq),
        grid=(n + 1,),
        in_specs=[pl.BlockSpec(memory_space=pltpu.SMEM),
                  pl.BlockSpec((tm, d), done), pl.BlockSpec((tm, ob.shape[1]), done),
                  pl.BlockSpec((tm, gate.shape[1]), done),
                  pl.BlockSpec((tm, qw), ahead(q_col // qw)),
                  pl.BlockSpec((BLOCK, kvw), before(k_col // kvw)),
                  pl.BlockSpec((tm, kvw), ahead(k_col // kvw)),
                  pl.BlockSpec((BLOCK, kvw), before(v_col // kvw)),
                  pl.BlockSpec((tm, kvw), ahead(v_col // kvw)),
                  _resident(bias.shape), _resident(band.shape),
                  _resident(wa.shape), _resident(wb.shape), _resident(wo.shape),
                  _resident(g2.shape), _resident(w1.shape), _resident(w3.shape),
                  _resident(w2.shape), _resident(gf.shape)],
        out_specs=pl.BlockSpec((tm, d), done),
        out_shape=jax.ShapeDtypeStruct((t, d), F32),
        scratch_shapes=[pltpu.VMEM((tm, d), F32), pltpu.VMEM((tm, d), BF16),
                        pltpu.VMEM((tm, qw), BF16),
                        pltpu.VMEM((tm + BLOCK, kvw), BF16), pltpu.VMEM((tm + BLOCK, kvw), BF16)],
        compiler_params=pltpu.CompilerParams(
            dimension_semantics=("arbitrary",), vmem_limit_bytes=VMEM_LIMIT),
        name="post",
    )(sinks, x, ob, gate, qkv, qkv, qkv, qkv, qkv, bias, band, wa, wb, wo, g2, w1, w3, w2, gf)


class _swa_stages:
    def __init__(self, sink_ref, q_ref, kprev_ref, k_ref, vprev_ref, v_ref, bias_ref, band_ref,
                 kwin_ref, vwin_ref, first_rows):
        kwin_ref[0:BLOCK, :] = kprev_ref[...]
        kwin_ref[BLOCK:, :] = k_ref[...]
        vwin_ref[0:BLOCK, :] = vprev_ref[...]
        vwin_ref[BLOCK:, :] = v_ref[...]
        self.q_ref, self.kwin_ref, self.vwin_ref, self.bias_ref = q_ref, kwin_ref, vwin_ref, bias_ref
        self.blocks = q_ref.shape[0] // BLOCK
        lane = lax.broadcasted_iota(jnp.int32, (1, LANES), 1)
        self.low_half = lane < HEAD_DIM
        self.sels = (self.low_half, jnp.logical_not(self.low_half))
        key_col = lax.broadcasted_iota(jnp.int32, (1, 2 * BLOCK), 1)
        band = band_ref[...] > 0.0
        self.valid = [band] * self.blocks
        self.valid[0] = jnp.logical_and(
            band, jnp.logical_or(key_col >= BLOCK, jnp.logical_not(first_rows)))
        self.fills = [jnp.where(key_col == 0, sink_ref[hd], NEG_BIG) for hd in range(SWA_Q_HEADS)]
        not_row0 = lax.broadcasted_iota(jnp.int32, (2 * BLOCK, 1), 0) != 0
        self.vsels = tuple(jnp.logical_and(not_row0, sel) for sel in self.sels)
        self.probs = {}

    def logits(self):
        out = []
        for n in range(self.blocks):
            kw = self.kwin_ref[n * BLOCK:(n + 2) * BLOCK, :]
            for hd in range(SWA_Q_HEADS):
                p, e = divmod(hd, HEADS_PER_TILE)
                q = self.q_ref[n * BLOCK:(n + 1) * BLOCK, p * LANES:(p + 1) * LANES]
                out.append(_dot_nt(jnp.where(self.sels[e], q, jnp.zeros_like(q)), kw))
        return out

    def softmax_block(self, n, logits):
        self.probs[n] = []
        for hd in range(SWA_Q_HEADS):
            lg = jnp.where(self.valid[n], logits[n * SWA_Q_HEADS + hd] + self.bias_ref[hd],
                           self.fills[hd])
            pr = jnp.exp(lg - jnp.max(lg, axis=-1, keepdims=True))
            self.probs[n].append((pr.astype(BF16), 1.0 / jnp.sum(pr, axis=-1, keepdims=True)))

    def pv_block(self, n, o_ref):
        probs = self.probs[n]
        vw = self.vwin_ref[n * BLOCK:(n + 2) * BLOCK, :]
        vhalf = [jnp.where(sel, vw, jnp.zeros_like(vw)) for sel in self.vsels]
        for p in range(SWA_Q_HEADS // HEADS_PER_TILE):
            acc = jnp.zeros((BLOCK, LANES), F32)
            for e in range(HEADS_PER_TILE):
                acc = acc + _dot(probs[p * HEADS_PER_TILE + e][0], vhalf[e])
            inv = [probs[p * HEADS_PER_TILE + e][1] for e in range(HEADS_PER_TILE)]
            o_ref[n * BLOCK:(n + 1) * BLOCK, p * LANES:(p + 1) * LANES] = (
                acc * jnp.where(self.low_half, inv[0], inv[1])).astype(o_ref.dtype)


def _sb_kernel(q_ref, k_ref, v_ref, u_ref, o_ref):
    seq = q_ref.shape[0]
    lane = lax.broadcasted_iota(jnp.int32, (1, LANES), 1)
    low_half = lane < HEAD_DIM
    rows = lax.broadcasted_iota(jnp.int32, (SB_TQ, SB_UNIT), 0)
    cols = lax.broadcasted_iota(jnp.int32, (SB_TQ, SB_UNIT), 1)
    causal = jnp.concatenate([cols < rows] * HEADS_PER_TILE, axis=0)
    u = u_ref[...]

    def unit(x, i):
        return x[:, i * SB_UNIT:(i + 1) * SB_UNIT]

    def sweep(items):
        w2s, lhs = [], []
        for q2, start, n_u, diag, _, _ in items:
            keys = k_ref[pl.ds(start, n_u * SB_UNIT), :]
            w2 = _dot_nt(q2, keys) * LOG2E
            if diag:
                last = jnp.where(causal, unit(w2, n_u - 1), -NEG_BIG)
                w2 = jnp.concatenate([unit(w2, i) for i in range(n_u - 1)] + [last], axis=1)
            w2s.append(w2)
        for (_, _, n_u, _, _, _), w2 in zip(items, w2s):
            neg_abs = lax.bitcast_convert_type(
                lax.bitcast_convert_type(w2, jnp.uint32) | jnp.uint32(0x80000000), F32)
            log_keep = jnp.minimum(w2, 0.0) - jnp.log2(1.0 + jnp.exp2(neg_abs))
            hi = log_keep.astype(BF16)
            lo = (log_keep - hi.astype(F32)).astype(BF16)
            lhs += [jnp.concatenate([unit(hi, i), unit(lo, i)], axis=1) for i in range(n_u)]
        sums = _dot(jnp.concatenate(lhs, axis=0), u)
        weights, runnings = [], []
        row = 0
        for (_, _, n_u, _, running, _), w2 in zip(items, w2s):
            a = [None] * n_u
            for i in reversed(range(n_u)):
                part = sums[row + i * HEADS_PER_TILE * SB_TQ:row + (i + 1) * HEADS_PER_TILE * SB_TQ]
                local, total = unit(part, 0), unit(part, 1)
                c = local if running is None else local + running
                running = total if running is None else running + total
                a[i] = jnp.exp2(c - unit(w2, i)).astype(BF16)
            row += n_u * HEADS_PER_TILE * SB_TQ
            weights.append(jnp.concatenate(a, axis=1))
            runnings.append(running)
        out = []
        for (_, start, n_u, _, _, acc), a, running in zip(items, weights, runnings):
            both = _dot(a, v_ref[pl.ds(start, n_u * SB_UNIT), :])
            mine = jnp.where(low_half, both[:SB_TQ], both[SB_TQ:])
            out.append((running, mine if acc is None else acc + mine))
        return out

    def stacked_q(i):
        q = q_ref[pl.ds(pl.multiple_of(i * SB_TQ, SB_TQ), SB_TQ), :]
        return jnp.concatenate([jnp.where(low_half, q, jnp.zeros_like(q)),
                                jnp.where(low_half, jnp.zeros_like(q), q)], axis=0)

    def alive(r):
        return jnp.max(r) > SB_LOG2_UNDERFLOW

    def finish(i, q2, running, acc):
        def cond(state):
            return jnp.logical_and(state[0] >= 0, state[1])

        def body(state):
            ku, _, r, acc_ = state
            (r, acc_), = sweep([(q2, pl.multiple_of(ku * SB_UNIT, SB_UNIT), 1, False, r, acc_)])
            return ku - 1, alive(r), r, acc_

        return lax.while_loop(cond, body, (i - SB_WINDOW_UNITS, alive(running), running, acc))[3]

    def qblocks(idx, n_u):
        items = []
        for i in idx:
            first = pl.multiple_of((i - (n_u - 1)) * SB_UNIT, SB_UNIT)
            items.append((stacked_q(i), first, n_u, True, None, None))
        swept = sweep(items)
        accs = [acc for _, acc in swept]
        if n_u == SB_WINDOW_UNITS:
            worst = functools.reduce(jnp.maximum, [running for running, _ in swept])
            accs = lax.cond(
                alive(worst),
                lambda: [finish(i, item[0], running, acc)
                         for i, item, (running, acc) in zip(idx, items, swept)],
                lambda: accs)
        for i, acc in zip(idx, accs):
            o_ref[pl.ds(pl.multiple_of(i * SB_TQ, SB_TQ), SB_TQ), :] = acc.astype(o_ref.dtype)

    n_q = seq // SB_TQ
    n_lead = SB_WINDOW_UNITS - 1
    for i in range(n_lead):
        qblocks([i], i + 1)

    def step(j, carry):
        i = n_lead + SB_GROUP * j
        qblocks([i + g for g in range(SB_GROUP)], SB_WINDOW_UNITS)
        return carry

    n_groups = (n_q - n_lead) // SB_GROUP
    lax.fori_loop(0, n_groups, step, 0)
    tail = list(range(n_lead + n_groups * SB_GROUP, n_q))
    if tail:
        qblocks(tail, SB_WINDOW_UNITS)


def _sb(qkv, u, *, batch, seq, q_col, k_col, v_col):
    t = qkv.shape[0]
    pairs = SB_HEADS // HEADS_PER_TILE

    def col(c0):
        return lambda b, p: (b, c0 // LANES + p)

    return pl.pallas_call(
        _sb_kernel,
        grid=(batch, pairs),
        in_specs=[
            pl.BlockSpec((seq, LANES), col(q_col)),
            pl.BlockSpec((seq, LANES), col(k_col)),
            pl.BlockSpec((seq, LANES), col(v_col)),
            _resident(u.shape),
        ],
        out_specs=pl.BlockSpec((seq, LANES), lambda b, p: (b, p)),
        out_shape=jax.ShapeDtypeStruct((t, SB_HEADS * HEAD_DIM), BF16),
        compiler_params=pltpu.CompilerParams(
            dimension_semantics=("arbitrary", "arbitrary"), vmem_limit_bytes=VMEM_LIMIT),
        name="sb",
    )(qkv, qkv, qkv, u)


def _rel_bucket(dist):
    max_exact = REL_BUCKETS // 2
    d = jnp.maximum(dist, 1).astype(F32)
    large = max_exact + (jnp.log(d / max_exact) / math.log(REL_MAX_DIST / max_exact)
                         * (REL_BUCKETS - max_exact)).astype(jnp.int32)
    large = jnp.minimum(large, REL_BUCKETS - 1)
    return jnp.where(dist < max_exact, dist, large)


def _swa_bias(rel_table):
    f = rel_table.astype(F32)[_rel_bucket(jnp.arange(SWA_WINDOW))].T
    heads = f.shape[0]
    span = 3 * BLOCK
    v = jnp.pad(f, ((0, 0), (BLOCK - 1, span - SWA_WINDOW - (BLOCK - 1))))
    shifted = jnp.tile(v, (1, BLOCK + 1))[:, :BLOCK * (span + 1)].reshape(heads, BLOCK, span + 1)
    return shifted[:, :, :2 * BLOCK][:, :, ::-1]


def _band_mask():
    qi = np.arange(BLOCK)[:, None] + BLOCK
    kj = np.arange(2 * BLOCK)[None, :]
    dist = qi - kj
    return jnp.asarray(((dist >= 0) & (dist < SWA_WINDOW)).astype(np.float32))


def _swa_head_order(w, axis, width=HEAD_DIM):
    shape = w.shape
    w = w.reshape(shape[:axis] + (SWA_KV_HEADS, SWA_Q_HEADS // SWA_KV_HEADS, width) + shape[axis + 1:])
    return jnp.swapaxes(w, axis, axis + 1).reshape(shape)


def kernel(x, norm_ffn1, ffn1_w1, ffn1_w3, ffn1_w2, norm_mix, w_in, swa_sinks, rel_bias,
           w_branch_swa, w_branch_sb, w_out, norm_ffn2, ffn2_w1, ffn2_w3, ffn2_w2, norm_final):
    batch, seq, d = x.shape
    depth = norm_ffn1.shape[0]
    qa_w = SWA_Q_HEADS * HEAD_DIM
    kva_w = SWA_KV_HEADS * HEAD_DIM
    sb_w = SB_HEADS * HEAD_DIM
    scale = HEAD_DIM ** -0.5

    bias, band = _swa_head_order(_swa_bias(rel_bias), 0, 1), _band_mask()
    tri = (jnp.arange(SB_UNIT)[:, None] >= jnp.arange(SB_UNIT)[None, :]).astype(BF16)
    half = jnp.concatenate([tri, jnp.ones_like(tri)], axis=1)
    u = jnp.concatenate([half, half], axis=0)
    gain_final = norm_final.reshape(1, d)
    c_qa, c_ka, c_va = 0, qa_w, qa_w + kva_w
    c_qb = qa_w + 2 * kva_w
    c_kb, c_vb = c_qb + sb_w, c_qb + 2 * sb_w

    xt = x.reshape(batch * seq, d)
    for layer in range(depth):
        w = w_in[layer]
        o = 0
        cols = {}
        for name, width in (("qa", qa_w), ("ka", kva_w), ("va", kva_w), ("qb", sb_w), ("kb", sb_w),
                            ("vb", sb_w), ("g", 2 * d)):
            cols[name] = w[:, o:o + width]
            o += width
        wqkv = jnp.concatenate([_swa_head_order(cols["qa"], 1) * scale, cols["ka"], cols["va"],
                                cols["qb"] * -scale, cols["kb"], cols["vb"]], axis=1).astype(BF16)
        x1, qkv, gate = _pre(xt, norm_ffn1[layer].reshape(1, d), ffn1_w1[layer].astype(BF16),
                             ffn1_w3[layer].astype(BF16), ffn1_w2[layer].astype(BF16),
                             norm_mix[layer].reshape(1, d), wqkv, cols["g"].astype(BF16))
        ob = _sb(qkv, u, batch=batch, seq=seq, q_col=c_qb, k_col=c_kb, v_col=c_vb)
        xt = _post(x1, ob, gate, qkv, _swa_head_order(swa_sinks[layer], 0, 1), bias, band,
                   _swa_head_order(w_branch_swa[layer], 0).astype(BF16),
                   w_branch_sb[layer].astype(BF16),
                   w_out[layer].astype(BF16), norm_ffn2[layer].reshape(1, d),
                   ffn2_w1[layer].astype(BF16), ffn2_w3[layer].astype(BF16),
                   ffn2_w2[layer].astype(BF16), gain_final, final_norm=layer == depth - 1,
                   seq=seq, q_col=c_qa, k_col=c_ka, v_col=c_va)
    return xt.reshape(batch, seq, d)
```

```python
import functools
import math

import jax
import jax.numpy as jnp
import numpy as np
from jax import lax
from jax.experimental import pallas as pl
from jax.experimental.pallas import tpu as pltpu

F32 = jnp.float32
BF16 = jnp.bfloat16

HEAD_DIM = 64
SWA_Q_HEADS = 8
SWA_KV_HEADS = 2
SWA_WINDOW = 128
SB_HEADS = 8
BLOCK = 128
REL_BUCKETS = 32
REL_MAX_DIST = 128
RMS_EPS = 1e-6
NEG_BIG = -1e30

LANES = 128
HEADS_PER_TILE = LANES // HEAD_DIM
FFN_CHUNK = 256
PROJ_CHUNK = 256
SWA_FIRST_CHUNK = 1
SWA_CHUNK_STRIDE = 2
SWA_PV_LAG = 2
TOKEN_TILE = 512
SB_TQ = 128
SB_UNIT = 128
SB_WINDOW_UNITS = 3
SB_GROUP = 16
SB_STAGE_LAG = 1
LOG2E = math.log2(math.e)
SB_LOG2_UNDERFLOW = -150.0
VMEM_LIMIT = 60000 * 1024


def _rms(x, g):
    ms = jnp.mean(x * x, axis=-1, keepdims=True)
    return x * lax.rsqrt(ms + RMS_EPS) * g


def _dot(a, b):
    return jnp.dot(a, b, preferred_element_type=F32)


def _dot_nt(a, b):
    return lax.dot_general(a, b, (((1,), (1,)), ((), ())), preferred_element_type=F32)


def _resident(shape):
    nd = len(shape)
    return pl.BlockSpec(shape, lambda *_: (0,) * nd, pipeline_mode=pl.Buffered(1))


def _swiglu_into(acc_ref, h, w1_ref, w3_ref, w2_ref, before_chunk=None):
    d_ff = w1_ref.shape[1]
    for c in range(d_ff // FFN_CHUNK):
        for work in (before_chunk or {}).get(c, ()):
            work()
        cols = slice(c * FFN_CHUNK, (c + 1) * FFN_CHUNK)
        a = _dot(h, w1_ref[:, cols])
        b = _dot(h, w3_ref[:, cols])
        gated = (a * jax.nn.sigmoid(a) * b).astype(BF16)
        down = _dot(gated, w2_ref[cols, :])
        if c == 0:
            acc_ref[...] = down
        else:
            acc_ref[...] += down


def _pre_kernel(x_ref, g1_ref, w1_ref, w3_ref, w2_ref, gm_ref, wqkv_ref, wg_ref,
                x1_ref, qkv_ref, gate_ref, acc_ref):
    x = x_ref[...]
    _swiglu_into(acc_ref, _rms(x, g1_ref[...]).astype(BF16), w1_ref, w3_ref, w2_ref)
    x1 = x + 0.5 * acc_ref[...]
    x1_ref[...] = x1
    h = _rms(x1, gm_ref[...]).astype(BF16)
    n_qkv = qkv_ref.shape[1] // PROJ_CHUNK
    for c in range(n_qkv):
        cols = slice(c * PROJ_CHUNK, (c + 1) * PROJ_CHUNK)
        if c == n_qkv - 1 and n_qkv % 2:
            half = h.shape[0] // 2
            for rows in (slice(0, half), slice(half, None)):
                qkv_ref[rows, cols] = _dot(h[rows], wqkv_ref[:, cols]).astype(BF16)
        else:
            qkv_ref[:, cols] = _dot(h, wqkv_ref[:, cols]).astype(BF16)
    for c in range(gate_ref.shape[1] // PROJ_CHUNK):
        cols = slice(c * PROJ_CHUNK, (c + 1) * PROJ_CHUNK)
        gate_ref[:, cols] = jax.nn.sigmoid(_dot(h, wg_ref[:, cols])).astype(BF16)


def _pre(x, g1, w1, w3, w2, gm, wqkv, wg):
    t, d = x.shape
    tm = TOKEN_TILE
    row = lambda i: (i, 0)
    return pl.pallas_call(
        _pre_kernel,
        grid=(t // tm,),
        in_specs=[pl.BlockSpec((tm, d), row), _resident(g1.shape), _resident(w1.shape),
                  _resident(w3.shape), _resident(w2.shape), _resident(gm.shape),
                  _resident(wqkv.shape), _resident(wg.shape)],
        out_specs=[pl.BlockSpec((tm, d), row), pl.BlockSpec((tm, wqkv.shape[1]), row),
                   pl.BlockSpec((tm, wg.shape[1]), row)],
        out_shape=[jax.ShapeDtypeStruct((t, d), F32),
                   jax.ShapeDtypeStruct((t, wqkv.shape[1]), BF16),
                   jax.ShapeDtypeStruct((t, wg.shape[1]), BF16)],
        scratch_shapes=[pltpu.VMEM((tm, d), F32)],
        compiler_params=pltpu.CompilerParams(
            dimension_semantics=("arbitrary",), vmem_limit_bytes=VMEM_LIMIT),
        name="pre",
    )(x, g1, w1, w3, w2, gm, wqkv, wg)


def _post_kernel(sink_ref, x_ref, ob_ref, gate_ref, q_ref, kprev_ref, k_ref, vprev_ref, v_ref,
                 bias_ref, band_ref, wa_ref, wb_ref, wo_ref, g2_ref, w1_ref, w3_ref, w2_ref, gf_ref,
                 o_ref, acc_ref, mrg_ref, oa_ref, kwin_ref, vwin_ref, *, final_norm, seq):
    s = pl.program_id(0)
    tm, d = x_ref.shape
    tile = jnp.minimum(s, pl.num_programs(0) - 2)
    first_rows = lax.rem(tile * tm, seq) == 0

    @pl.when(s == 0)
    def _():
        oa_ref[...] = jnp.zeros_like(oa_ref)

    attend = _swa_stages(sink_ref, q_ref, kprev_ref, k_ref, vprev_ref, v_ref, bias_ref, band_ref,
                         kwin_ref, vwin_ref, first_rows)
    logits = attend.logits()
    oa = oa_ref[...]
    ob = ob_ref[...]
    for c in range(d // PROJ_CHUNK):
        cols = slice(c * PROJ_CHUNK, (c + 1) * PROJ_CHUNK)
        gcols = slice(d + c * PROJ_CHUNK, d + (c + 1) * PROJ_CHUNK)
        merged = (gate_ref[:, cols].astype(F32) * _dot(oa, wa_ref[:, cols])
                  + gate_ref[:, gcols].astype(F32) * _dot(ob, wb_ref[:, cols]))
        mrg_ref[:, cols] = merged.astype(BF16)
    x2 = x_ref[...] + _dot(mrg_ref[...], wo_ref[...])
    spread = {}
    for n in range(attend.blocks):
        first = SWA_FIRST_CHUNK + SWA_CHUNK_STRIDE * n
        spread.setdefault(first, []).append(functools.partial(attend.softmax_block, n, logits))
        spread.setdefault(first + SWA_PV_LAG, []).append(
            functools.partial(attend.pv_block, n, oa_ref))
    _swiglu_into(acc_ref, _rms(x2, g2_ref[...]).astype(BF16), w1_ref, w3_ref, w2_ref,
                 before_chunk=spread)
    y = x2 + 0.5 * acc_ref[...]
    if final_norm:
        y = _rms(y, gf_ref[...])
    o_ref[...] = y


def _post(x, ob, gate, qkv, sinks, bias, band, wa, wb, wo, g2, w1, w3, w2, gf, *, final_norm, seq,
          q_col, k_col, v_col):
    t, d = x.shape
    tm = TOKEN_TILE
    n = t // tm
    qw = SWA_Q_HEADS * HEAD_DIM
    kvw = SWA_KV_HEADS * HEAD_DIM
    blocks_per_tile = tm // BLOCK
    done = lambda s: (jnp.maximum(s - 1, 0), 0)

    def ahead(col_block):
        return lambda s: (jnp.minimum(s, n - 1), col_block)

    def before(col_block):
        return lambda s: (jnp.maximum(jnp.minimum(s, n - 1) * blocks_per_tile - 1, 0), col_block)

    return pl.pallas_call(
        functools.partial(_post_kernel, final_norm=final_norm, seq=seq),
        grid=(n + 1,),
        in_specs=[pl.BlockSpec(memory_space=pltpu.SMEM),
                  pl.BlockSpec((tm, d), done), pl.BlockSpec((tm, ob.shape[1]), done),
                  pl.BlockSpec((tm, gate.shape[1]), done),
                  pl.BlockSpec((tm, qw), ahead(q_col // qw)),
                  pl.BlockSpec((BLOCK, kvw), before(k_col // kvw)),
                  pl.BlockSpec((tm, kvw), ahead(k_col // kvw)),
                  pl.BlockSpec((BLOCK, kvw), before(v_col // kvw)),
                  pl.BlockSpec((tm, kvw), ahead(v_col // kvw)),
                  _resident(bias.shape), _resident(band.shape),
                  _resident(wa.shape), _resident(wb.shape), _resident(wo.shape),
                  _resident(g2.shape), _resident(w1.shape), _resident(w3.shape),
                  _resident(w2.shape), _resident(gf.shape)],
        out_specs=pl.BlockSpec((tm, d), done),
        out_shape=jax.ShapeDtypeStruct((t, d), F32),
        scratch_shapes=[pltpu.VMEM((tm, d), F32), pltpu.VMEM((tm, d), BF16),
                        pltpu.VMEM((tm, qw), BF16),
                        pltpu.VMEM((tm + BLOCK, kvw), BF16), pltpu.VMEM((tm + BLOCK, kvw), BF16)],
        compiler_params=pltpu.CompilerParams(
            dimension_semantics=("arbitrary",), vmem_limit_bytes=VMEM_LIMIT),
        name="post",
    )(sinks, x, ob, gate, qkv, qkv, qkv, qkv, qkv, bias, band, wa, wb, wo, g2, w1, w3, w2, gf)


class _swa_stages:
    def __init__(self, sink_ref, q_ref, kprev_ref, k_ref, vprev_ref, v_ref, bias_ref, band_ref,
                 kwin_ref, vwin_ref, first_rows):
        kwin_ref[0:BLOCK, :] = kprev_ref[...]
        kwin_ref[BLOCK:, :] = k_ref[...]
        vwin_ref[0:BLOCK, :] = vprev_ref[...]
        vwin_ref[BLOCK:, :] = v_ref[...]
        self.q_ref, self.kwin_ref, self.vwin_ref, self.bias_ref = q_ref, kwin_ref, vwin_ref, bias_ref
        self.blocks = q_ref.shape[0] // BLOCK
        lane = lax.broadcasted_iota(jnp.int32, (1, LANES), 1)
        self.low_half = lane < HEAD_DIM
        self.sels = (self.low_half, jnp.logical_not(self.low_half))
        key_col = lax.broadcasted_iota(jnp.int32, (1, 2 * BLOCK), 1)
        band = band_ref[...] > 0.0
        self.valid = [band] * self.blocks
        self.valid[0] = jnp.logical_and(
            band, jnp.logical_or(key_col >= BLOCK, jnp.logical_not(first_rows)))
        self.fills = [jnp.where(key_col == 0, sink_ref[hd], NEG_BIG) for hd in range(SWA_Q_HEADS)]
        not_row0 = lax.broadcasted_iota(jnp.int32, (2 * BLOCK, 1), 0) != 0
        self.vsels = tuple(jnp.logical_and(not_row0, sel) for sel in self.sels)
        self.probs = {}

    def logits(self):
        out = []
        for n in range(self.blocks):
            kw = self.kwin_ref[n * BLOCK:(n + 2) * BLOCK, :]
            for hd in range(SWA_Q_HEADS):
                p, e = divmod(hd, HEADS_PER_TILE)
                q = self.q_ref[n * BLOCK:(n + 1) * BLOCK, p * LANES:(p + 1) * LANES]
                out.append(_dot_nt(jnp.where(self.sels[e], q, jnp.zeros_like(q)), kw))
        return out

    def softmax_block(self, n, logits):
        self.probs[n] = []
        for hd in range(SWA_Q_HEADS):
            lg = jnp.where(self.valid[n], logits[n * SWA_Q_HEADS + hd] + self.bias_ref[hd],
                           self.fills[hd])
            pr = jnp.exp(lg - jnp.max(lg, axis=-1, keepdims=True))
            self.probs[n].append((pr.astype(BF16), 1.0 / jnp.sum(pr, axis=-1, keepdims=True)))

    def pv_block(self, n, o_ref):
        probs = self.probs[n]
        vw = self.vwin_ref[n * BLOCK:(n + 2) * BLOCK, :]
        vhalf = [jnp.where(sel, vw, jnp.zeros_like(vw)) for sel in self.vsels]
        for p in range(SWA_Q_HEADS // HEADS_PER_TILE):
            acc = jnp.zeros((BLOCK, LANES), F32)
            for e in range(HEADS_PER_TILE):
                acc = acc + _dot(probs[p * HEADS_PER_TILE + e][0], vhalf[e])
            inv = [probs[p * HEADS_PER_TILE + e][1] for e in range(HEADS_PER_TILE)]
            o_ref[n * BLOCK:(n + 1) * BLOCK, p * LANES:(p + 1) * LANES] = (
                acc * jnp.where(self.low_half, inv[0], inv[1])).astype(o_ref.dtype)


def _sb_kernel(q_ref, k_ref, v_ref, u_ref, o_ref):
    seq = q_ref.shape[0]
    lane = lax.broadcasted_iota(jnp.int32, (1, LANES), 1)
    low_half = lane < HEAD_DIM
    rows = lax.broadcasted_iota(jnp.int32, (SB_TQ, SB_UNIT), 0)
    cols = lax.broadcasted_iota(jnp.int32, (SB_TQ, SB_UNIT), 1)
    causal = jnp.concatenate([cols < rows] * HEADS_PER_TILE, axis=0)
    u = u_ref[...]

    def unit(x, i):
        return x[:, i * SB_UNIT:(i + 1) * SB_UNIT]

    def sweep(items):
        n = len(items)
        rows2 = HEADS_PER_TILE * SB_TQ
        w2s, lhs, weights, runnings, out = [None] * n, [None] * n, [None] * n, [None] * n, [None] * n

        def logits_stage(t):
            q2, start, n_u, diag, _, _ = items[t]
            keys = k_ref[pl.ds(start, n_u * SB_UNIT), :]
            w2 = _dot_nt(q2, keys)
            if diag:
                last = jnp.where(causal, unit(w2, n_u - 1), -NEG_BIG)
                w2 = jnp.concatenate([unit(w2, i) for i in range(n_u - 1)] + [last], axis=1)
            neg_abs = lax.bitcast_convert_type(
                lax.bitcast_convert_type(w2, jnp.uint32) | jnp.uint32(0x80000000), F32)
            log_keep = jnp.minimum(w2, 0.0) - jnp.log2(1.0 + jnp.exp2(neg_abs))
            hi = log_keep.astype(BF16)
            lo = (log_keep - hi.astype(F32)).astype(BF16)
            w2s[t] = w2
            lhs[t] = jnp.concatenate(
                [jnp.concatenate([unit(hi, i), unit(lo, i)], axis=1) for i in range(n_u)], axis=0)

        def sums_stage(t):
            _, _, n_u, _, running, _ = items[t]
            sums = _dot(lhs[t], u)
            a = [None] * n_u
            for i in reversed(range(n_u)):
                part = sums[i * rows2:(i + 1) * rows2]
                local, total = unit(part, 0), unit(part, 1)
                c = local if running is None else local + running
                running = total if running is None else running + total
                a[i] = jnp.exp2(c - unit(w2s[t], i)).astype(BF16)
            weights[t] = jnp.concatenate(a, axis=1)
            runnings[t] = running

        def output_stage(t):
            _, start, n_u, _, _, acc = items[t]
            both = _dot(weights[t], v_ref[pl.ds(start, n_u * SB_UNIT), :])
            mine = jnp.where(low_half, both[:SB_TQ], both[SB_TQ:])
            out[t] = (runnings[t], mine if acc is None else acc + mine)

        for t in range(n + 2 * SB_STAGE_LAG):
            for stage, first in ((logits_stage, t), (sums_stage, t - SB_STAGE_LAG),
                                 (output_stage, t - 2 * SB_STAGE_LAG)):
                if 0 <= first < n:
                    stage(first)
        return out

    def stacked_q(i):
        q = q_ref[pl.ds(pl.multiple_of(i * SB_TQ, SB_TQ), SB_TQ), :]
        return jnp.concatenate([jnp.where(low_half, q, jnp.zeros_like(q)),
                                jnp.where(low_half, jnp.zeros_like(q), q)], axis=0)

    def alive(r):
        return jnp.max(r) > SB_LOG2_UNDERFLOW

    def finish(i, q2, running, acc):
        def cond(state):
            return jnp.logical_and(state[0] >= 0, state[1])

        def body(state):
            ku, _, r, acc_ = state
            (r, acc_), = sweep([(q2, pl.multiple_of(ku * SB_UNIT, SB_UNIT), 1, False, r, acc_)])
            return ku - 1, alive(r), r, acc_

        return lax.while_loop(cond, body, (i - SB_WINDOW_UNITS, alive(running), running, acc))[3]

    def qblocks(idx, n_u):
        items = []
        for i in idx:
            first = pl.multiple_of((i - (n_u - 1)) * SB_UNIT, SB_UNIT)
            items.append((stacked_q(i), first, n_u, True, None, None))
        swept = sweep(items)
        accs = [acc for _, acc in swept]
        if n_u == SB_WINDOW_UNITS:
            worst = functools.reduce(jnp.maximum, [running for running, _ in swept])
            accs = lax.cond(
                alive(worst),
                lambda: [finish(i, item[0], running, acc)
                         for i, item, (running, acc) in zip(idx, items, swept)],
                lambda: accs)
        for i, acc in zip(idx, accs):
            o_ref[pl.ds(pl.multiple_of(i * SB_TQ, SB_TQ), SB_TQ), :] = acc.astype(o_ref.dtype)

    n_q = seq // SB_TQ
    n_lead = SB_WINDOW_UNITS - 1
    for i in range(n_lead):
        qblocks([i], i + 1)

    def step(j, carry):
        i = n_lead + SB_GROUP * j
        qblocks([i + g for g in range(SB_GROUP)], SB_WINDOW_UNITS)
        return carry

    n_groups = (n_q - n_lead) // SB_GROUP
    lax.fori_loop(0, n_groups, step, 0)
    tail = list(range(n_lead + n_groups * SB_GROUP, n_q))
    if tail:
        qblocks(tail, SB_WINDOW_UNITS)


def _sb(qkv, u, *, batch, seq, q_col, k_col, v_col):
    t = qkv.shape[0]
    pairs = SB_HEADS // HEADS_PER_TILE

    def col(c0):
        return lambda b, p: (b, c0 // LANES + p)

    return pl.pallas_call(
        _sb_kernel,
        grid=(batch, pairs),
        in_specs=[
            pl.BlockSpec((seq, LANES), col(q_col)),
            pl.BlockSpec((seq, LANES), col(k_col)),
            pl.BlockSpec((seq, LANES), col(v_col)),
            _resident(u.shape),
        ],
        out_specs=pl.BlockSpec((seq, LANES), lambda b, p: (b, p)),
        out_shape=jax.ShapeDtypeStruct((t, SB_HEADS * HEAD_DIM), BF16),
        compiler_params=pltpu.CompilerParams(
            dimension_semantics=("arbitrary", "arbitrary"), vmem_limit_bytes=VMEM_LIMIT),
        name="sb",
    )(qkv, qkv, qkv, u)


def _rel_bucket(dist):
    max_exact = REL_BUCKETS // 2
    d = jnp.maximum(dist, 1).astype(F32)
    large = max_exact + (jnp.log(d / max_exact) / math.log(REL_MAX_DIST / max_exact)
                         * (REL_BUCKETS - max_exact)).astype(jnp.int32)
    large = jnp.minimum(large, REL_BUCKETS - 1)
    return jnp.where(dist < max_exact, dist, large)


def _swa_bias(rel_table):
    f = rel_table.astype(F32)[_rel_bucket(jnp.arange(SWA_WINDOW))].T
    heads = f.shape[0]
    span = 3 * BLOCK
    v = jnp.pad(f, ((0, 0), (BLOCK - 1, span - SWA_WINDOW - (BLOCK - 1))))
    shifted = jnp.tile(v, (1, BLOCK + 1))[:, :BLOCK * (span + 1)].reshape(heads, BLOCK, span + 1)
    return shifted[:, :, :2 * BLOCK][:, :, ::-1]


def _band_mask():
    qi = np.arange(BLOCK)[:, None] + BLOCK
    kj = np.arange(2 * BLOCK)[None, :]
    dist = qi - kj
    return jnp.asarray(((dist >= 0) & (dist < SWA_WINDOW)).astype(np.float32))


def _swa_head_order(w, axis, width=HEAD_DIM):
    shape = w.shape
    w = w.reshape(shape[:axis] + (SWA_KV_HEADS, SWA_Q_HEADS // SWA_KV_HEADS, width) + shape[axis + 1:])
    return jnp.swapaxes(w, axis, axis + 1).reshape(shape)


def kernel(x, norm_ffn1, ffn1_w1, ffn1_w3, ffn1_w2, norm_mix, w_in, swa_sinks, rel_bias,
           w_branch_swa, w_branch_sb, w_out, norm_ffn2, ffn2_w1, ffn2_w3, ffn2_w2, norm_final):
    batch, seq, d = x.shape
    depth = norm_ffn1.shape[0]
    qa_w = SWA_Q_HEADS * HEAD_DIM
    kva_w = SWA_KV_HEADS * HEAD_DIM
    sb_w = SB_HEADS * HEAD_DIM
    scale = HEAD_DIM ** -0.5

    bias, band = _swa_head_order(_swa_bias(rel_bias), 0, 1), _band_mask()
    tri = (jnp.arange(SB_UNIT)[:, None] >= jnp.arange(SB_UNIT)[None, :]).astype(BF16)
    half = jnp.concatenate([tri, jnp.ones_like(tri)], axis=1)
    u = jnp.concatenate([half, half], axis=0)
    gain_final = norm_final.reshape(1, d)
    c_qa, c_ka, c_va = 0, qa_w, qa_w + kva_w
    c_qb = qa_w + 2 * kva_w
    c_kb, c_vb = c_qb + sb_w, c_qb + 2 * sb_w

    xt = x.reshape(batch * seq, d)
    for layer in range(depth):
        w = w_in[layer]
        o = 0
        cols = {}
        for name, width in (("qa", qa_w), ("ka", kva_w), ("va", kva_w), ("qb", sb_w), ("kb", sb_w),
                            ("vb", sb_w), ("g", 2 * d)):
            cols[name] = w[:, o:o + width]
            o += width
        wqkv = jnp.concatenate([_swa_head_order(cols["qa"], 1) * scale, cols["ka"], cols["va"],
                                cols["qb"] * (-scale * LOG2E), cols["kb"], cols["vb"]],
                               axis=1).astype(BF16)
        x1, qkv, gate = _pre(xt, norm_ffn1[layer].reshape(1, d), ffn1_w1[layer].astype(BF16),
                             ffn1_w3[layer].astype(BF16), ffn1_w2[layer].astype(BF16),
                             norm_mix[layer].reshape(1, d), wqkv, cols["g"].astype(BF16))
        ob = _sb(qkv, u, batch=batch, seq=seq, q_col=c_qb, k_col=c_kb, v_col=c_vb)
        xt = _post(x1, ob, gate, qkv, _swa_head_order(swa_sinks[layer], 0, 1), bias, band,
                   _swa_head_order(w_branch_swa[layer], 0).astype(BF16),
                   w_branch_sb[layer].astype(BF16),
                   w_out[layer].astype(BF16), norm_ffn2[layer].reshape(1, d),
                   ffn2_w1[layer].astype(BF16), ffn2_w3[layer].astype(BF16),
                   ffn2_w2[layer].astype(BF16), gain_final, final_norm=layer == depth - 1,
                   seq=seq, q_col=c_qa, k_col=c_ka, v_col=c_va)
    return xt.reshape(batch, seq, d)
```

```python
import functools
import math

import jax
import jax.numpy as jnp
import numpy as np
from jax import lax
from jax.experimental import pallas as pl
from jax.experimental.pallas import tpu as pltpu

F32 = jnp.float32
BF16 = jnp.bfloat16

HEAD_DIM = 64
SWA_Q_HEADS = 8
SWA_KV_HEADS = 2
SWA_WINDOW = 128
SB_HEADS = 8
BLOCK = 128
REL_BUCKETS = 32
REL_MAX_DIST = 128
RMS_EPS = 1e-6
NEG_BIG = -1e30

LANES = 128
HEADS_PER_TILE = LANES // HEAD_DIM
FFN_CHUNK = 256
PROJ_CHUNK = 256
SWA_FIRST_CHUNK = 1
SWA_PV_LAG = 2
PRE_TILE = 1024
POST_TILE = 512
SB_TQ = 128
SB_UNIT = 128
SB_WINDOW_UNITS = 3
SB_GROUP = 16
SB_STAGE_LAG = 1
LOG2E = math.log2(math.e)
SB_LOG2_UNDERFLOW = -150.0
VMEM_LIMIT = 60000 * 1024


def _rms(x, g):
    ms = jnp.mean(x * x, axis=-1, keepdims=True)
    return x * lax.rsqrt(ms + RMS_EPS) * g


def _dot(a, b):
    return jnp.dot(a, b, preferred_element_type=F32)


def _dot_nt(a, b):
    return lax.dot_general(a, b, (((1,), (1,)), ((), ())), preferred_element_type=F32)


def _resident(shape):
    nd = len(shape)
    return pl.BlockSpec(shape, lambda *_: (0,) * nd, pipeline_mode=pl.Buffered(1))


def _swiglu_into(acc_ref, h, w1_ref, w3_ref, w2_ref, before_chunk=None):
    d_ff = w1_ref.shape[1]
    for c in range(d_ff // FFN_CHUNK):
        for work in (before_chunk or {}).get(c, ()):
            work()
        cols = slice(c * FFN_CHUNK, (c + 1) * FFN_CHUNK)
        a = _dot(h, w1_ref[:, cols])
        b = _dot(h, w3_ref[:, cols])
        gated = (a * jax.nn.sigmoid(a) * b).astype(BF16)
        down = _dot(gated, w2_ref[cols, :])
        if c == 0:
            acc_ref[...] = down
        else:
            acc_ref[...] += down


def _pre_kernel(x_ref, g1_ref, w1_ref, w3_ref, w2_ref, gm_ref, wqkv_ref,
                x1_ref, qkv_ref, acc_ref):
    x = x_ref[...]
    _swiglu_into(acc_ref, _rms(x, g1_ref[...]).astype(BF16), w1_ref, w3_ref, w2_ref)
    x1 = x + 0.5 * acc_ref[...]
    x1_ref[...] = x1
    h = _rms(x1, gm_ref[...]).astype(BF16)
    n_qkv = qkv_ref.shape[1] // PROJ_CHUNK
    for c in range(n_qkv):
        cols = slice(c * PROJ_CHUNK, (c + 1) * PROJ_CHUNK)
        if c == n_qkv - 1 and n_qkv % 2:
            half = h.shape[0] // 2
            for rows in (slice(0, half), slice(half, None)):
                qkv_ref[rows, cols] = _dot(h[rows], wqkv_ref[:, cols]).astype(BF16)
        else:
            qkv_ref[:, cols] = _dot(h, wqkv_ref[:, cols]).astype(BF16)


def _pre(x, g1, w1, w3, w2, gm, wqkv):
    t, d = x.shape
    tm = PRE_TILE
    row = lambda i: (i, 0)
    return pl.pallas_call(
        _pre_kernel,
        grid=(t // tm,),
        in_specs=[pl.BlockSpec((tm, d), row), _resident(g1.shape), _resident(w1.shape),
                  _resident(w3.shape), _resident(w2.shape), _resident(gm.shape),
                  _resident(wqkv.shape)],
        out_specs=[pl.BlockSpec((tm, d), row), pl.BlockSpec((tm, wqkv.shape[1]), row)],
        out_shape=[jax.ShapeDtypeStruct((t, d), F32),
                   jax.ShapeDtypeStruct((t, wqkv.shape[1]), BF16)],
        scratch_shapes=[pltpu.VMEM((tm, d), F32)],
        compiler_params=pltpu.CompilerParams(
            dimension_semantics=("arbitrary",), vmem_limit_bytes=VMEM_LIMIT),
        name="pre",
    )(x, g1, w1, w3, w2, gm, wqkv)


def _post_kernel(sink_ref, x_ref, ob_ref, q_ref, kprev_ref, k_ref, vprev_ref, v_ref,
                 bias_ref, band_ref, gm_ref, wg_ref, wa_ref, wb_ref, wo_ref, g2_ref, w1_ref, w3_ref,
                 w2_ref, gf_ref, o_ref, acc_ref, mrg_ref, oa_ref, kwin_ref, vwin_ref, *,
                 final_norm, seq):
    s = pl.program_id(0)
    tm, d = x_ref.shape
    tile = jnp.minimum(s, pl.num_programs(0) - 2)
    first_rows = lax.rem(tile * tm, seq) == 0

    @pl.when(s == 0)
    def _():
        oa_ref[...] = jnp.zeros_like(oa_ref)

    attend = _swa_stages(sink_ref, q_ref, kprev_ref, k_ref, vprev_ref, v_ref, bias_ref, band_ref,
                         kwin_ref, vwin_ref, first_rows)
    logits = attend.logits()
    oa = oa_ref[...]
    ob = ob_ref[...]
    x1 = x_ref[...]
    h = _rms(x1, gm_ref[...]).astype(BF16)
    for c in range(d // PROJ_CHUNK):
        cols = slice(c * PROJ_CHUNK, (c + 1) * PROJ_CHUNK)
        gcols = slice(d + c * PROJ_CHUNK, d + (c + 1) * PROJ_CHUNK)
        merged = (jax.nn.sigmoid(_dot(h, wg_ref[:, cols])) * _dot(oa, wa_ref[:, cols])
                  + jax.nn.sigmoid(_dot(h, wg_ref[:, gcols])) * _dot(ob, wb_ref[:, cols]))
        mrg_ref[:, cols] = merged.astype(BF16)
    x2 = x1 + _dot(mrg_ref[...], wo_ref[...])
    spread = {}
    n_chunks = w1_ref.shape[1] // FFN_CHUNK
    stride = max(1, (n_chunks - 1 - SWA_FIRST_CHUNK - SWA_PV_LAG) // max(attend.blocks - 1, 1))
    for n in range(attend.blocks):
        first = SWA_FIRST_CHUNK + stride * n
        spread.setdefault(first, []).append(functools.partial(attend.softmax_block, n, logits))
        spread.setdefault(first + SWA_PV_LAG, []).append(
            functools.partial(attend.pv_block, n, oa_ref))
    _swiglu_into(acc_ref, _rms(x2, g2_ref[...]).astype(BF16), w1_ref, w3_ref, w2_ref,
                 before_chunk=spread)
    y = x2 + 0.5 * acc_ref[...]
    if final_norm:
        y = _rms(y, gf_ref[...])
    o_ref[...] = y


def _post(x, ob, qkv, sinks, bias, band, gm, wg, wa, wb, wo, g2, w1, w3, w2, gf, *, final_norm,
          seq, q_col, k_col, v_col):
    t, d = x.shape
    tm = POST_TILE
    n = t // tm
    qw = SWA_Q_HEADS * HEAD_DIM
    kvw = SWA_KV_HEADS * HEAD_DIM
    blocks_per_tile = tm // BLOCK
    done = lambda s: (jnp.maximum(s - 1, 0), 0)

    def ahead(col_block):
        return lambda s: (jnp.minimum(s, n - 1), col_block)

    def before(col_block):
        return lambda s: (jnp.maximum(jnp.minimum(s, n - 1) * blocks_per_tile - 1, 0), col_block)

    return pl.pallas_call(
        functools.partial(_post_kernel, final_norm=final_norm, seq=seq),
        grid=(n + 1,),
        in_specs=[pl.BlockSpec(memory_space=pltpu.SMEM),
                  pl.BlockSpec((tm, d), done), pl.BlockSpec((tm, ob.shape[1]), done),
                  pl.BlockSpec((tm, qw), ahead(q_col // qw)),
                  pl.BlockSpec((BLOCK, kvw), before(k_col // kvw)),
                  pl.BlockSpec((tm, kvw), ahead(k_col // kvw)),
                  pl.BlockSpec((BLOCK, kvw), before(v_col // kvw)),
                  pl.BlockSpec((tm, kvw), ahead(v_col // kvw)),
                  _resident(bias.shape), _resident(band.shape),
                  _resident(gm.shape), _resident(wg.shape),
                  _resident(wa.shape), _resident(wb.shape), _resident(wo.shape),
                  _resident(g2.shape), _resident(w1.shape), _resident(w3.shape),
                  _resident(w2.shape), _resident(gf.shape)],
        out_specs=pl.BlockSpec((tm, d), done),
        out_shape=jax.ShapeDtypeStruct((t, d), F32),
        scratch_shapes=[pltpu.VMEM((tm, d), F32), pltpu.VMEM((tm, d), BF16),
                        pltpu.VMEM((tm, qw), BF16),
                        pltpu.VMEM((tm + BLOCK, kvw), BF16), pltpu.VMEM((tm + BLOCK, kvw), BF16)],
        compiler_params=pltpu.CompilerParams(
            dimension_semantics=("arbitrary",), vmem_limit_bytes=VMEM_LIMIT),
        name="post",
    )(sinks, x, ob, qkv, qkv, qkv, qkv, qkv, bias, band, gm, wg, wa, wb, wo, g2, w1, w3, w2, gf)


class _swa_stages:
    def __init__(self, sink_ref, q_ref, kprev_ref, k_ref, vprev_ref, v_ref, bias_ref, band_ref,
                 kwin_ref, vwin_ref, first_rows):
        kwin_ref[0:BLOCK, :] = kprev_ref[...]
        kwin_ref[BLOCK:, :] = k_ref[...]
        vwin_ref[0:BLOCK, :] = vprev_ref[...]
        vwin_ref[BLOCK:, :] = v_ref[...]
        self.q_ref, self.kwin_ref, self.vwin_ref, self.bias_ref = q_ref, kwin_ref, vwin_ref, bias_ref
        self.blocks = q_ref.shape[0] // BLOCK
        lane = lax.broadcasted_iota(jnp.int32, (1, LANES), 1)
        self.low_half = lane < HEAD_DIM
        self.sels = (self.low_half, jnp.logical_not(self.low_half))
        key_col = lax.broadcasted_iota(jnp.int32, (1, 2 * BLOCK), 1)
        band = band_ref[...] > 0.0
        self.valid = [band] * self.blocks
        self.valid[0] = jnp.logical_and(
            band, jnp.logical_or(key_col >= BLOCK, jnp.logical_not(first_rows)))
        self.fills = [jnp.where(key_col == 0, sink_ref[hd], NEG_BIG) for hd in range(SWA_Q_HEADS)]
        not_row0 = lax.broadcasted_iota(jnp.int32, (2 * BLOCK, 1), 0) != 0
        self.vsels = tuple(jnp.logical_and(not_row0, sel) for sel in self.sels)
        self.probs = {}

    def logits(self):
        out = []
        for n in range(self.blocks):
            kw = self.kwin_ref[n * BLOCK:(n + 2) * BLOCK, :]
            for hd in range(SWA_Q_HEADS):
                p, e = divmod(hd, HEADS_PER_TILE)
                q = self.q_ref[n * BLOCK:(n + 1) * BLOCK, p * LANES:(p + 1) * LANES]
                out.append(_dot_nt(jnp.where(self.sels[e], q, jnp.zeros_like(q)), kw))
        return out

    def softmax_block(self, n, logits):
        self.probs[n] = []
        for hd in range(SWA_Q_HEADS):
            lg = jnp.where(self.valid[n], logits[n * SWA_Q_HEADS + hd] + self.bias_ref[hd],
                           self.fills[hd])
            pr = jnp.exp(lg - jnp.max(lg, axis=-1, keepdims=True))
            self.probs[n].append((pr.astype(BF16), 1.0 / jnp.sum(pr, axis=-1, keepdims=True)))

    def pv_block(self, n, o_ref):
        probs = self.probs[n]
        vw = self.vwin_ref[n * BLOCK:(n + 2) * BLOCK, :]
        vhalf = [jnp.where(sel, vw, jnp.zeros_like(vw)) for sel in self.vsels]
        for p in range(SWA_Q_HEADS // HEADS_PER_TILE):
            acc = jnp.zeros((BLOCK, LANES), F32)
            for e in range(HEADS_PER_TILE):
                acc = acc + _dot(probs[p * HEADS_PER_TILE + e][0], vhalf[e])
            inv = [probs[p * HEADS_PER_TILE + e][1] for e in range(HEADS_PER_TILE)]
            o_ref[n * BLOCK:(n + 1) * BLOCK, p * LANES:(p + 1) * LANES] = (
                acc * jnp.where(self.low_half, inv[0], inv[1])).astype(o_ref.dtype)


def _sb_kernel(q_ref, k_ref, v_ref, u_ref, o_ref):
    seq = q_ref.shape[0]
    lane = lax.broadcasted_iota(jnp.int32, (1, LANES), 1)
    low_half = lane < HEAD_DIM
    rows = lax.broadcasted_iota(jnp.int32, (SB_TQ, SB_UNIT), 0)
    cols = lax.broadcasted_iota(jnp.int32, (SB_TQ, SB_UNIT), 1)
    causal = jnp.concatenate([cols < rows] * HEADS_PER_TILE, axis=0)
    u = u_ref[...]

    def unit(x, i):
        return x[:, i * SB_UNIT:(i + 1) * SB_UNIT]

    def sweep(items):
        n = len(items)
        rows2 = HEADS_PER_TILE * SB_TQ
        w2s, lhs, weights, runnings, out = [None] * n, [None] * n, [None] * n, [None] * n, [None] * n

        def logits_stage(t):
            q2, start, n_u, diag, _, _ = items[t]
            keys = k_ref[pl.ds(start, n_u * SB_UNIT), :]
            w2 = _dot_nt(q2, keys)
            if diag:
                last = jnp.where(causal, unit(w2, n_u - 1), -NEG_BIG)
                w2 = jnp.concatenate([unit(w2, i) for i in range(n_u - 1)] + [last], axis=1)
            neg_abs = lax.bitcast_convert_type(
                lax.bitcast_convert_type(w2, jnp.uint32) | jnp.uint32(0x80000000), F32)
            log_keep = jnp.minimum(w2, 0.0) - jnp.log2(1.0 + jnp.exp2(neg_abs))
            hi = log_keep.astype(BF16)
            lo = (log_keep - hi.astype(F32)).astype(BF16)
            w2s[t] = w2
            lhs[t] = jnp.concatenate(
                [jnp.concatenate([unit(hi, i), unit(lo, i)], axis=1) for i in range(n_u)], axis=0)

        def sums_stage(t):
            _, _, n_u, _, running, _ = items[t]
            sums = _dot(lhs[t], u)
            a = [None] * n_u
            for i in reversed(range(n_u)):
                part = sums[i * rows2:(i + 1) * rows2]
                local, total = unit(part, 0), unit(part, 1)
                c = local if running is None else local + running
                running = total if running is None else running + total
                a[i] = jnp.exp2(c - unit(w2s[t], i)).astype(BF16)
            weights[t] = jnp.concatenate(a, axis=1)
            runnings[t] = running

        def output_stage(t):
            _, start, n_u, _, _, acc = items[t]
            both = _dot(weights[t], v_ref[pl.ds(start, n_u * SB_UNIT), :])
            mine = jnp.where(low_half, both[:SB_TQ], both[SB_TQ:])
            out[t] = (runnings[t], mine if acc is None else acc + mine)

        for t in range(n + 2 * SB_STAGE_LAG):
            for stage, first in ((logits_stage, t), (sums_stage, t - SB_STAGE_LAG),
                                 (output_stage, t - 2 * SB_STAGE_LAG)):
                if 0 <= first < n:
                    stage(first)
        return out

    def stacked_q(i):
        q = q_ref[pl.ds(pl.multiple_of(i * SB_TQ, SB_TQ), SB_TQ), :]
        return jnp.concatenate([jnp.where(low_half, q, jnp.zeros_like(q)),
                                jnp.where(low_half, jnp.zeros_like(q), q)], axis=0)

    def alive(r):
        return jnp.max(r) > SB_LOG2_UNDERFLOW

    def finish(i, q2, running, acc):
        def cond(state):
            return jnp.logical_and(state[0] >= 0, state[1])

        def body(state):
            ku, _, r, acc_ = state
            (r, acc_), = sweep([(q2, pl.multiple_of(ku * SB_UNIT, SB_UNIT), 1, False, r, acc_)])
            return ku - 1, alive(r), r, acc_

        return lax.while_loop(cond, body, (i - SB_WINDOW_UNITS, alive(running), running, acc))[3]

    def qblocks(idx, n_u):
        items = []
        for i in idx:
            first = pl.multiple_of((i - (n_u - 1)) * SB_UNIT, SB_UNIT)
            items.append((stacked_q(i), first, n_u, True, None, None))
        swept = sweep(items)
        accs = [acc for _, acc in swept]
        if n_u == SB_WINDOW_UNITS:
            worst = functools.reduce(jnp.maximum, [running for running, _ in swept])
            accs = lax.cond(
                alive(worst),
                lambda: [finish(i, item[0], running, acc)
                         for i, item, (running, acc) in zip(idx, items, swept)],
                lambda: accs)
        for i, acc in zip(idx, accs):
            o_ref[pl.ds(pl.multiple_of(i * SB_TQ, SB_TQ), SB_TQ), :] = acc.astype(o_ref.dtype)

    n_q = seq // SB_TQ
    n_lead = SB_WINDOW_UNITS - 1
    for i in range(n_lead):
        qblocks([i], i + 1)

    def step(j, carry):
        i = n_lead + SB_GROUP * j
        qblocks([i + g for g in range(SB_GROUP)], SB_WINDOW_UNITS)
        return carry

    n_groups = (n_q - n_lead) // SB_GROUP
    lax.fori_loop(0, n_groups, step, 0)
    tail = list(range(n_lead + n_groups * SB_GROUP, n_q))
    if tail:
        qblocks(tail, SB_WINDOW_UNITS)


def _sb(qkv, u, *, batch, seq, q_col, k_col, v_col):
    t = qkv.shape[0]
    pairs = SB_HEADS // HEADS_PER_TILE

    def col(c0):
        return lambda b, p: (b, c0 // LANES + p)

    return pl.pallas_call(
        _sb_kernel,
        grid=(batch, pairs),
        in_specs=[
            pl.BlockSpec((seq, LANES), col(q_col)),
            pl.BlockSpec((seq, LANES), col(k_col)),
            pl.BlockSpec((seq, LANES), col(v_col)),
            _resident(u.shape),
        ],
        out_specs=pl.BlockSpec((seq, LANES), lambda b, p: (b, p)),
        out_shape=jax.ShapeDtypeStruct((t, SB_HEADS * HEAD_DIM), BF16),
        compiler_params=pltpu.CompilerParams(
            dimension_semantics=("arbitrary", "arbitrary"), vmem_limit_bytes=VMEM_LIMIT),
        name="sb",
    )(qkv, qkv, qkv, u)


def _rel_bucket(dist):
    max_exact = REL_BUCKETS // 2
    d = jnp.maximum(dist, 1).astype(F32)
    large = max_exact + (jnp.log(d / max_exact) / math.log(REL_MAX_DIST / max_exact)
                         * (REL_BUCKETS - max_exact)).astype(jnp.int32)
    large = jnp.minimum(large, REL_BUCKETS - 1)
    return jnp.where(dist < max_exact, dist, large)


def _swa_bias(rel_table):
    f = rel_table.astype(F32)[_rel_bucket(jnp.arange(SWA_WINDOW))].T
    heads = f.shape[0]
    span = 3 * BLOCK
    v = jnp.pad(f, ((0, 0), (BLOCK - 1, span - SWA_WINDOW - (BLOCK - 1))))
    shifted = jnp.tile(v, (1, BLOCK + 1))[:, :BLOCK * (span + 1)].reshape(heads, BLOCK, span + 1)
    return shifted[:, :, :2 * BLOCK][:, :, ::-1]


def _band_mask():
    qi = np.arange(BLOCK)[:, None] + BLOCK
    kj = np.arange(2 * BLOCK)[None, :]
    dist = qi - kj
    return jnp.asarray(((dist >= 0) & (dist < SWA_WINDOW)).astype(np.float32))


def _swa_head_order(w, axis, width=HEAD_DIM):
    shape = w.shape
    w = w.reshape(shape[:axis] + (SWA_KV_HEADS, SWA_Q_HEADS // SWA_KV_HEADS, width) + shape[axis + 1:])
    return jnp.swapaxes(w, axis, axis + 1).reshape(shape)


def kernel(x, norm_ffn1, ffn1_w1, ffn1_w3, ffn1_w2, norm_mix, w_in, swa_sinks, rel_bias,
           w_branch_swa, w_branch_sb, w_out, norm_ffn2, ffn2_w1, ffn2_w3, ffn2_w2, norm_final):
    batch, seq, d = x.shape
    depth = norm_ffn1.shape[0]
    qa_w = SWA_Q_HEADS * HEAD_DIM
    kva_w = SWA_KV_HEADS * HEAD_DIM
    sb_w = SB_HEADS * HEAD_DIM
    scale = HEAD_DIM ** -0.5

    bias, band = _swa_head_order(_swa_bias(rel_bias), 0, 1), _band_mask()
    tri = (jnp.arange(SB_UNIT)[:, None] >= jnp.arange(SB_UNIT)[None, :]).astype(BF16)
    half = jnp.concatenate([tri, jnp.ones_like(tri)], axis=1)
    u = jnp.concatenate([half, half], axis=0)
    gain_final = norm_final.reshape(1, d)
    c_qa, c_ka, c_va = 0, qa_w, qa_w + kva_w
    c_qb = qa_w + 2 * kva_w
    c_kb, c_vb = c_qb + sb_w, c_qb + 2 * sb_w

    xt = x.reshape(batch * seq, d)
    for layer in range(depth):
        w = w_in[layer]
        o = 0
        cols = {}
        for name, width in (("qa", qa_w), ("ka", kva_w), ("va", kva_w), ("qb", sb_w), ("kb", sb_w),
                            ("vb", sb_w), ("g", 2 * d)):
            cols[name] = w[:, o:o + width]
            o += width
        wqkv = jnp.concatenate([_swa_head_order(cols["qa"], 1) * scale, cols["ka"], cols["va"],
                                cols["qb"] * (-scale * LOG2E), cols["kb"], cols["vb"]],
                               axis=1).astype(BF16)
        gain_mix = norm_mix[layer].reshape(1, d)
        x1, qkv = _pre(xt, norm_ffn1[layer].reshape(1, d), ffn1_w1[layer].astype(BF16),
                       ffn1_w3[layer].astype(BF16), ffn1_w2[layer].astype(BF16), gain_mix, wqkv)
        ob = _sb(qkv, u, batch=batch, seq=seq, q_col=c_qb, k_col=c_kb, v_col=c_vb)
        xt = _post(x1, ob, qkv, _swa_head_order(swa_sinks[layer], 0, 1), bias, band,
                   gain_mix, cols["g"].astype(BF16),
                   _swa_head_order(w_branch_swa[layer], 0).astype(BF16),
                   w_branch_sb[layer].astype(BF16),
                   w_out[layer].astype(BF16), norm_ffn2[layer].reshape(1, d),
                   ffn2_w1[layer].astype(BF16), ffn2_w3[layer].astype(BF16),
                   ffn2_w2[layer].astype(BF16), gain_final, final_norm=layer == depth - 1,
                   seq=seq, q_col=c_qa, k_col=c_ka, v_col=c_va)
    return xt.reshape(batch, seq, d)
```

```python
import functools
import math

import jax
import jax.numpy as jnp
import numpy as np
from jax import lax
from jax.experimental import pallas as pl
from jax.experimental.pallas import tpu as pltpu

F32 = jnp.float32
BF16 = jnp.bfloat16

HEAD_DIM = 64
SWA_Q_HEADS = 8
SWA_KV_HEADS = 2
SWA_WINDOW = 128
SB_HEADS = 8
BLOCK = 128
REL_BUCKETS = 32
REL_MAX_DIST = 128
RMS_EPS = 1e-6
NEG_BIG = -1e30

LANES = 128
HEADS_PER_TILE = LANES // HEAD_DIM
FFN_CHUNK = 256
PROJ_CHUNK = 256
SWA_FIRST_CHUNK = 1
SWA_PV_LAG = 2
PRE_TILE = 1024
POST_TILE = 512
SB_TQ = 128
SB_UNIT = 128
SB_WINDOW_UNITS = 3
SB_GROUP = 16
SB_STAGE_LAG = 1
LOG2E = math.log2(math.e)
SB_LOG2_UNDERFLOW = -150.0
VMEM_LIMIT = 60000 * 1024


def _rms(x, g):
    ms = jnp.mean(x * x, axis=-1, keepdims=True)
    return x * lax.rsqrt(ms + RMS_EPS) * g


def _dot(a, b):
    return jnp.dot(a, b, preferred_element_type=F32)


def _dot_nt(a, b):
    return lax.dot_general(a, b, (((1,), (1,)), ((), ())), preferred_element_type=F32)


def _resident(shape):
    nd = len(shape)
    return pl.BlockSpec(shape, lambda *_: (0,) * nd, pipeline_mode=pl.Buffered(1))


def _swiglu_into(acc_ref, h, w1_ref, w3_ref, w2_ref, before_chunk=None):
    d_ff = w1_ref.shape[1]
    for c in range(d_ff // FFN_CHUNK):
        for work in (before_chunk or {}).get(c, ()):
            work()
        cols = slice(c * FFN_CHUNK, (c + 1) * FFN_CHUNK)
        a = _dot(h, w1_ref[:, cols])
        b = _dot(h, w3_ref[:, cols])
        gated = (a * jax.nn.sigmoid(a) * b).astype(BF16)
        down = _dot(gated, w2_ref[cols, :])
        if c == 0:
            acc_ref[...] = down
        else:
            acc_ref[...] += down


def _pre_kernel(x_ref, g1_ref, w1_ref, w3_ref, w2_ref, gm_ref, wqkv_ref,
                x1_ref, qkv_ref, acc_ref):
    x = x_ref[...]
    _swiglu_into(acc_ref, _rms(x, g1_ref[...]).astype(BF16), w1_ref, w3_ref, w2_ref)
    x1 = x + 0.5 * acc_ref[...]
    x1_ref[...] = x1
    h = _rms(x1, gm_ref[...]).astype(BF16)
    n_qkv = qkv_ref.shape[1] // PROJ_CHUNK
    for c in range(n_qkv):
        cols = slice(c * PROJ_CHUNK, (c + 1) * PROJ_CHUNK)
        if c == n_qkv - 1 and n_qkv % 2:
            half = h.shape[0] // 2
            for rows in (slice(0, half), slice(half, None)):
                qkv_ref[rows, cols] = _dot(h[rows], wqkv_ref[:, cols]).astype(BF16)
        else:
            qkv_ref[:, cols] = _dot(h, wqkv_ref[:, cols]).astype(BF16)


def _pre(x, g1, w1, w3, w2, gm, wqkv):
    t, d = x.shape
    tm = PRE_TILE
    row = lambda i: (i, 0)
    return pl.pallas_call(
        _pre_kernel,
        grid=(t // tm,),
        in_specs=[pl.BlockSpec((tm, d), row), _resident(g1.shape), _resident(w1.shape),
                  _resident(w3.shape), _resident(w2.shape), _resident(gm.shape),
                  _resident(wqkv.shape)],
        out_specs=[pl.BlockSpec((tm, d), row), pl.BlockSpec((tm, wqkv.shape[1]), row)],
        out_shape=[jax.ShapeDtypeStruct((t, d), F32),
                   jax.ShapeDtypeStruct((t, wqkv.shape[1]), BF16)],
        scratch_shapes=[pltpu.VMEM((tm, d), F32)],
        compiler_params=pltpu.CompilerParams(
            dimension_semantics=("arbitrary",), vmem_limit_bytes=VMEM_LIMIT),
        name="pre",
    )(x, g1, w1, w3, w2, gm, wqkv)


def _post_kernel(sink_ref, x_ref, ob_ref, q_ref, kprev_ref, k_ref, vprev_ref, v_ref,
                 bias_ref, band_ref, gm_ref, wg_ref, wa_ref, wb_ref, wo_ref, g2_ref, w1_ref, w3_ref,
                 w2_ref, gf_ref, o_ref, acc_ref, mrg_ref, oa_ref, kwin_ref, vwin_ref, *,
                 final_norm, seq):
    s = pl.program_id(0)
    tm, d = x_ref.shape
    tile = jnp.minimum(s, pl.num_programs(0) - 2)
    first_rows = lax.rem(tile * tm, seq) == 0

    @pl.when(s == 0)
    def _():
        oa_ref[...] = jnp.zeros_like(oa_ref)

    attend = _swa_stages(sink_ref, q_ref, kprev_ref, k_ref, vprev_ref, v_ref, bias_ref, band_ref,
                         kwin_ref, vwin_ref, first_rows)
    logits = attend.logits()
    oa = oa_ref[...]
    ob = ob_ref[...]
    x1 = x_ref[...]
    h = _rms(x1, gm_ref[...]).astype(BF16)
    for c in range(d // PROJ_CHUNK):
        cols = slice(c * PROJ_CHUNK, (c + 1) * PROJ_CHUNK)
        gcols = slice(d + c * PROJ_CHUNK, d + (c + 1) * PROJ_CHUNK)
        merged = (jax.nn.sigmoid(_dot(h, wg_ref[:, cols])) * _dot(oa, wa_ref[:, cols])
                  + jax.nn.sigmoid(_dot(h, wg_ref[:, gcols])) * _dot(ob, wb_ref[:, cols]))
        mrg_ref[:, cols] = merged.astype(BF16)
    x2 = x1 + _dot(mrg_ref[...], wo_ref[...])
    spread = {}
    n_chunks = w1_ref.shape[1] // FFN_CHUNK
    stride = max(1, (n_chunks - 1 - SWA_FIRST_CHUNK - SWA_PV_LAG) // max(attend.blocks - 1, 1))
    for n in range(attend.blocks):
        first = SWA_FIRST_CHUNK + stride * n
        spread.setdefault(first, []).append(functools.partial(attend.softmax_block, n, logits))
        spread.setdefault(first + SWA_PV_LAG, []).append(
            functools.partial(attend.pv_block, n, oa_ref))
    _swiglu_into(acc_ref, _rms(x2, g2_ref[...]).astype(BF16), w1_ref, w3_ref, w2_ref,
                 before_chunk=spread)
    y = x2 + 0.5 * acc_ref[...]
    if final_norm:
        y = _rms(y, gf_ref[...])
    o_ref[...] = y


def _post(x, ob, qkv, sinks, bias, band, gm, wg, wa, wb, wo, g2, w1, w3, w2, gf, *, final_norm,
          seq, q_col, k_col, v_col):
    t, d = x.shape
    tm = POST_TILE
    n = t // tm
    qw = SWA_Q_HEADS * HEAD_DIM
    kvw = SWA_KV_HEADS * HEAD_DIM
    blocks_per_tile = tm // BLOCK
    done = lambda s: (jnp.maximum(s - 1, 0), 0)

    def ahead(col_block):
        return lambda s: (jnp.minimum(s, n - 1), col_block)

    def before(col_block):
        return lambda s: (jnp.maximum(jnp.minimum(s, n - 1) * blocks_per_tile - 1, 0), col_block)

    return pl.pallas_call(
        functools.partial(_post_kernel, final_norm=final_norm, seq=seq),
        grid=(n + 1,),
        in_specs=[pl.BlockSpec(memory_space=pltpu.SMEM),
                  pl.BlockSpec((tm, d), done), pl.BlockSpec((tm, ob.shape[1]), done),
                  pl.BlockSpec((tm, qw), ahead(q_col // qw)),
                  pl.BlockSpec((BLOCK, kvw), before(k_col // kvw)),
                  pl.BlockSpec((tm, kvw), ahead(k_col // kvw)),
                  pl.BlockSpec((BLOCK, kvw), before(v_col // kvw)),
                  pl.BlockSpec((tm, kvw), ahead(v_col // kvw)),
                  _resident(bias.shape), _resident(band.shape),
                  _resident(gm.shape), _resident(wg.shape),
                  _resident(wa.shape), _resident(wb.shape), _resident(wo.shape),
                  _resident(g2.shape), _resident(w1.shape), _resident(w3.shape),
                  _resident(w2.shape), _resident(gf.shape)],
        out_specs=pl.BlockSpec((tm, d), done),
        out_shape=jax.ShapeDtypeStruct((t, d), F32),
        scratch_shapes=[pltpu.VMEM((tm, d), F32), pltpu.VMEM((tm, d), BF16),
                        pltpu.VMEM((tm, qw), BF16),
                        pltpu.VMEM((tm + BLOCK, kvw), BF16), pltpu.VMEM((tm + BLOCK, kvw), BF16)],
        compiler_params=pltpu.CompilerParams(
            dimension_semantics=("arbitrary",), vmem_limit_bytes=VMEM_LIMIT),
        name="post",
    )(sinks, x, ob, qkv, qkv, qkv, qkv, qkv, bias, band, gm, wg, wa, wb, wo, g2, w1, w3, w2, gf)


class _swa_stages:
    def __init__(self, sink_ref, q_ref, kprev_ref, k_ref, vprev_ref, v_ref, bias_ref, band_ref,
                 kwin_ref, vwin_ref, first_rows):
        kwin_ref[0:BLOCK, :] = kprev_ref[...]
        kwin_ref[BLOCK:, :] = k_ref[...]
        vwin_ref[0:BLOCK, :] = vprev_ref[...]
        vwin_ref[BLOCK:, :] = v_ref[...]
        self.q_ref, self.kwin_ref, self.vwin_ref, self.bias_ref = q_ref, kwin_ref, vwin_ref, bias_ref
        self.blocks = q_ref.shape[0] // BLOCK
        lane = lax.broadcasted_iota(jnp.int32, (1, LANES), 1)
        self.low_half = lane < HEAD_DIM
        self.sels = (self.low_half, jnp.logical_not(self.low_half))
        key_col = lax.broadcasted_iota(jnp.int32, (1, 2 * BLOCK), 1)
        band = band_ref[...] > 0.0
        self.valid = [band] * self.blocks
        self.valid[0] = jnp.logical_and(
            band, jnp.logical_or(key_col >= BLOCK, jnp.logical_not(first_rows)))
        self.fills = [jnp.where(key_col == 0, sink_ref[hd], NEG_BIG) for hd in range(SWA_Q_HEADS)]
        not_row0 = lax.broadcasted_iota(jnp.int32, (2 * BLOCK, 1), 0) != 0
        self.vsels = tuple(jnp.logical_and(not_row0, sel) for sel in self.sels)
        self.probs = {}

    def logits(self):
        out = []
        for n in range(self.blocks):
            kw = self.kwin_ref[n * BLOCK:(n + 2) * BLOCK, :]
            for hd in range(SWA_Q_HEADS):
                p, e = divmod(hd, HEADS_PER_TILE)
                q = self.q_ref[n * BLOCK:(n + 1) * BLOCK, p * LANES:(p + 1) * LANES]
                out.append(_dot_nt(jnp.where(self.sels[e], q, jnp.zeros_like(q)), kw))
        return out

    def softmax_block(self, n, logits):
        self.probs[n] = []
        for hd in range(SWA_Q_HEADS):
            lg = jnp.where(self.valid[n], logits[n * SWA_Q_HEADS + hd] + self.bias_ref[hd],
                           self.fills[hd])
            pr = jnp.exp(lg - jnp.max(lg, axis=-1, keepdims=True))
            self.probs[n].append((pr.astype(BF16), 1.0 / jnp.sum(pr, axis=-1, keepdims=True)))

    def pv_block(self, n, o_ref):
        probs = self.probs[n]
        vw = self.vwin_ref[n * BLOCK:(n + 2) * BLOCK, :]
        vhalf = [jnp.where(sel, vw, jnp.zeros_like(vw)) for sel in self.vsels]
        for p in range(SWA_Q_HEADS // HEADS_PER_TILE):
            acc = jnp.zeros((BLOCK, LANES), F32)
            for e in range(HEADS_PER_TILE):
                acc = acc + _dot(probs[p * HEADS_PER_TILE + e][0], vhalf[e])
            inv = [probs[p * HEADS_PER_TILE + e][1] for e in range(HEADS_PER_TILE)]
            o_ref[n * BLOCK:(n + 1) * BLOCK, p * LANES:(p + 1) * LANES] = (
                acc * jnp.where(self.low_half, inv[0], inv[1])).astype(o_ref.dtype)


def _sb_kernel(q_ref, k_ref, v_ref, u_ref, o_ref):
    seq = q_ref.shape[0]
    half = SB_TQ // 2
    rows2 = HEADS_PER_TILE * SB_TQ
    top = slice(0, rows2 // 2)
    bottom = slice(rows2 // 2, rows2)
    lane = lax.broadcasted_iota(jnp.int32, (1, LANES), 1)
    low_half = lane < HEAD_DIM
    rows = lax.broadcasted_iota(jnp.int32, (SB_TQ, SB_UNIT), 0)
    cols = lax.broadcasted_iota(jnp.int32, (SB_TQ, SB_UNIT), 1)
    earlier = cols < rows
    causal = jnp.concatenate([earlier[:half]] * HEADS_PER_TILE + [earlier[half:]] * HEADS_PER_TILE,
                             axis=0)
    in_top = lax.broadcasted_iota(jnp.int32, (rows2, 1), 0) < rows2 // 2
    u = u_ref[...]

    def unit(x, i):
        return x[:, i * SB_UNIT:(i + 1) * SB_UNIT]

    def softplus_parts(w2):
        neg_abs = lax.bitcast_convert_type(
            lax.bitcast_convert_type(w2, jnp.uint32) | jnp.uint32(0x80000000), F32)
        log_keep = jnp.minimum(w2, 0.0) - jnp.log2(1.0 + jnp.exp2(neg_abs))
        hi = log_keep.astype(BF16)
        return hi, (log_keep - hi.astype(F32)).astype(BF16)

    def sweep(items):
        n = len(items)
        pieces, lhs, weights, runnings, out = [None] * n, [None] * n, [None] * n, [None] * n, [None] * n

        def logits_stage(t):
            q2, start, n_u, diag, upper_far, skip_top, _, _ = items[t]
            keys = k_ref[pl.ds(start, n_u * SB_UNIT), :]
            w2 = _dot_nt(q2, keys)
            cut = []
            for i in range(n_u):
                part = unit(w2, i)
                if diag and i == n_u - 1:
                    part = jnp.where(causal, part, -NEG_BIG)
                if skip_top:
                    part = jnp.where(in_top, -NEG_BIG, part)
                rws = top if (upper_far and i == 0) else slice(0, rows2)
                cut.append((i, rws, part[rws]))
            pieces[t] = cut
            split = [softplus_parts(part) for _, _, part in cut]
            lhs[t] = jnp.concatenate([jnp.concatenate([hi, lo], axis=1) for hi, lo in split], axis=0)

        def sums_stage(t):
            n_u, running = items[t][2], items[t][6]
            sums = _dot(lhs[t], u)
            offsets = np.cumsum([0] + [rws.stop - rws.start for _, rws, _ in pieces[t]])
            a = [None] * n_u
            for (i, rws, part), off in reversed(list(zip(pieces[t], offsets))):
                prod = sums[off:off + rws.stop - rws.start]
                local, total = unit(prod, 0), unit(prod, 1)
                c = local if running is None else local + running[rws]
                if rws == top:
                    running = jnp.concatenate([running[top] + total, running[bottom]], axis=0)
                    a[i] = jnp.concatenate([jnp.exp2(c - part).astype(BF16),
                                            jnp.zeros((rows2 // 2, SB_UNIT), BF16)], axis=0)
                else:
                    running = total if running is None else running + total
                    a[i] = jnp.exp2(c - part).astype(BF16)
            weights[t] = jnp.concatenate(a, axis=1)
            runnings[t] = running

        def output_stage(t):
            start, n_u, acc = items[t][1], items[t][2], items[t][7]
            both = _dot(weights[t], v_ref[pl.ds(start, n_u * SB_UNIT), :])
            mine = jnp.concatenate(
                [jnp.where(low_half, both[0:half], both[half:2 * half]),
                 jnp.where(low_half, both[2 * half:3 * half], both[3 * half:])], axis=0)
            out[t] = (runnings[t], mine if acc is None else acc + mine)

        for t in range(n + 2 * SB_STAGE_LAG):
            for stage, first in ((logits_stage, t), (sums_stage, t - SB_STAGE_LAG),
                                 (output_stage, t - 2 * SB_STAGE_LAG)):
                if 0 <= first < n:
                    stage(first)
        return out

    def stacked_q(i):
        q = q_ref[pl.ds(pl.multiple_of(i * SB_TQ, SB_TQ), SB_TQ), :]
        zero = jnp.zeros_like(q[:half])
        return jnp.concatenate(
            [jnp.where(low_half, q[:half], zero), jnp.where(low_half, zero, q[:half]),
             jnp.where(low_half, q[half:], zero), jnp.where(low_half, zero, q[half:])], axis=0)

    def alive(r):
        return jnp.max(r) > SB_LOG2_UNDERFLOW

    def finish(i, q2, running, acc):
        far = pl.multiple_of((i - (SB_WINDOW_UNITS - 1)) * SB_UNIT, SB_UNIT)
        (running, acc), = sweep([(q2, far, 1, False, False, True, running, acc)])

        def cond(state):
            return jnp.logical_and(state[0] >= 0, state[1])

        def body(state):
            ku, _, r, acc_ = state
            first = pl.multiple_of(ku * SB_UNIT, SB_UNIT)
            (r, acc_), = sweep([(q2, first, 1, False, False, False, r, acc_)])
            return ku - 1, alive(r), r, acc_

        return lax.while_loop(cond, body, (i - SB_WINDOW_UNITS, alive(running), running, acc))[3]

    def qblocks(idx, n_u):
        full = n_u == SB_WINDOW_UNITS
        items = []
        for i in idx:
            first = pl.multiple_of((i - (n_u - 1)) * SB_UNIT, SB_UNIT)
            items.append((stacked_q(i), first, n_u, True, full, False, None, None))
        swept = sweep(items)
        accs = [acc for _, acc in swept]
        if full:
            worst = functools.reduce(jnp.maximum, [running for running, _ in swept])
            accs = lax.cond(
                alive(worst),
                lambda: [finish(i, item[0], running, acc)
                         for i, item, (running, acc) in zip(idx, items, swept)],
                lambda: accs)
        for i, acc in zip(idx, accs):
            o_ref[pl.ds(pl.multiple_of(i * SB_TQ, SB_TQ), SB_TQ), :] = acc.astype(o_ref.dtype)

    n_q = seq // SB_TQ
    n_lead = SB_WINDOW_UNITS - 1
    for i in range(n_lead):
        qblocks([i], i + 1)

    def step(j, carry):
        i = n_lead + SB_GROUP * j
        qblocks([i + g for g in range(SB_GROUP)], SB_WINDOW_UNITS)
        return carry

    n_groups = (n_q - n_lead) // SB_GROUP
    lax.fori_loop(0, n_groups, step, 0)
    tail = list(range(n_lead + n_groups * SB_GROUP, n_q))
    if tail:
        qblocks(tail, SB_WINDOW_UNITS)


def _sb(qkv, u, *, batch, seq, q_col, k_col, v_col):
    t = qkv.shape[0]
    pairs = SB_HEADS // HEADS_PER_TILE

    def col(c0):
        return lambda b, p: (b, c0 // LANES + p)

    return pl.pallas_call(
        _sb_kernel,
        grid=(batch, pairs),
        in_specs=[
            pl.BlockSpec((seq, LANES), col(q_col)),
            pl.BlockSpec((seq, LANES), col(k_col)),
            pl.BlockSpec((seq, LANES), col(v_col)),
            _resident(u.shape),
        ],
        out_specs=pl.BlockSpec((seq, LANES), lambda b, p: (b, p)),
        out_shape=jax.ShapeDtypeStruct((t, SB_HEADS * HEAD_DIM), BF16),
        compiler_params=pltpu.CompilerParams(
            dimension_semantics=("arbitrary", "arbitrary"), vmem_limit_bytes=VMEM_LIMIT),
        name="sb",
    )(qkv, qkv, qkv, u)


def _rel_bucket(dist):
    max_exact = REL_BUCKETS // 2
    d = jnp.maximum(dist, 1).astype(F32)
    large = max_exact + (jnp.log(d / max_exact) / math.log(REL_MAX_DIST / max_exact)
                         * (REL_BUCKETS - max_exact)).astype(jnp.int32)
    large = jnp.minimum(large, REL_BUCKETS - 1)
    return jnp.where(dist < max_exact, dist, large)


def _swa_bias(rel_table):
    f = rel_table.astype(F32)[_rel_bucket(jnp.arange(SWA_WINDOW))].T
    heads = f.shape[0]
    span = 3 * BLOCK
    v = jnp.pad(f, ((0, 0), (BLOCK - 1, span - SWA_WINDOW - (BLOCK - 1))))
    shifted = jnp.tile(v, (1, BLOCK + 1))[:, :BLOCK * (span + 1)].reshape(heads, BLOCK, span + 1)
    return shifted[:, :, :2 * BLOCK][:, :, ::-1]


def _band_mask():
    qi = np.arange(BLOCK)[:, None] + BLOCK
    kj = np.arange(2 * BLOCK)[None, :]
    dist = qi - kj
    return jnp.asarray(((dist >= 0) & (dist < SWA_WINDOW)).astype(np.float32))


def _swa_head_order(w, axis, width=HEAD_DIM):
    shape = w.shape
    w = w.reshape(shape[:axis] + (SWA_KV_HEADS, SWA_Q_HEADS // SWA_KV_HEADS, width) + shape[axis + 1:])
    return jnp.swapaxes(w, axis, axis + 1).reshape(shape)


def kernel(x, norm_ffn1, ffn1_w1, ffn1_w3, ffn1_w2, norm_mix, w_in, swa_sinks, rel_bias,
           w_branch_swa, w_branch_sb, w_out, norm_ffn2, ffn2_w1, ffn2_w3, ffn2_w2, norm_final):
    batch, seq, d = x.shape
    depth = norm_ffn1.shape[0]
    qa_w = SWA_Q_HEADS * HEAD_DIM
    kva_w = SWA_KV_HEADS * HEAD_DIM
    sb_w = SB_HEADS * HEAD_DIM
    scale = HEAD_DIM ** -0.5

    bias, band = _swa_head_order(_swa_bias(rel_bias), 0, 1), _band_mask()
    tri = (jnp.arange(SB_UNIT)[:, None] >= jnp.arange(SB_UNIT)[None, :]).astype(BF16)
    half = jnp.concatenate([tri, jnp.ones_like(tri)], axis=1)
    u = jnp.concatenate([half, half], axis=0)
    gain_final = norm_final.reshape(1, d)
    c_qa, c_ka, c_va = 0, qa_w, qa_w + kva_w
    c_qb = qa_w + 2 * kva_w
    c_kb, c_vb = c_qb + sb_w, c_qb + 2 * sb_w

    xt = x.reshape(batch * seq, d)
    for layer in range(depth):
        w = w_in[layer]
        o = 0
        cols = {}
        for name, width in (("qa", qa_w), ("ka", kva_w), ("va", kva_w), ("qb", sb_w), ("kb", sb_w),
                            ("vb", sb_w), ("g", 2 * d)):
            cols[name] = w[:, o:o + width]
            o += width
        wqkv = jnp.concatenate([_swa_head_order(cols["qa"], 1) * scale, cols["ka"], cols["va"],
                                cols["qb"] * (-scale * LOG2E), cols["kb"], cols["vb"]],
                               axis=1).astype(BF16)
        gain_mix = norm_mix[layer].reshape(1, d)
        x1, qkv = _pre(xt, norm_ffn1[layer].reshape(1, d), ffn1_w1[layer].astype(BF16),
                       ffn1_w3[layer].astype(BF16), ffn1_w2[layer].astype(BF16), gain_mix, wqkv)
        ob = _sb(qkv, u, batch=batch, seq=seq, q_col=c_qb, k_col=c_kb, v_col=c_vb)
        xt = _post(x1, ob, qkv, _swa_head_order(swa_sinks[layer], 0, 1), bias, band,
                   gain_mix, cols["g"].astype(BF16),
                   _swa_head_order(w_branch_swa[layer], 0).astype(BF16),
                   w_branch_sb[layer].astype(BF16),
                   w_out[layer].astype(BF16), norm_ffn2[layer].reshape(1, d),
                   ffn2_w1[layer].astype(BF16), ffn2_w3[layer].astype(BF16),
                   ffn2_w2[layer].astype(BF16), gain_final, final_norm=layer == depth - 1,
                   seq=seq, q_col=c_qa, k_col=c_ka, v_col=c_va)
    return xt.reshape(batch, seq, d)
```

```python
import functools
import math

import jax
import jax.numpy as jnp
import numpy as np
from jax import lax
from jax.experimental import pallas as pl
from jax.experimental.pallas import tpu as pltpu

F32 = jnp.float32
BF16 = jnp.bfloat16

HEAD_DIM = 64
SWA_Q_HEADS = 8
SWA_KV_HEADS = 2
SWA_WINDOW = 128
SB_HEADS = 8
BLOCK = 128
REL_BUCKETS = 32
REL_MAX_DIST = 128
RMS_EPS = 1e-6
NEG_BIG = -1e30

LANES = 128
HEADS_PER_TILE = LANES // HEAD_DIM
FFN_CHUNK = 256
PROJ_CHUNK = 256
SWA_FIRST_CHUNK = 1
SWA_PV_LAG = 2
PRE_TILE = 1024
POST_TILE = 512
SB_TQ = 128
SB_UNIT = 128
SB_WINDOW_UNITS = 3
SB_GROUP = 16
SB_STAGE_LAG = 1
LOG2E = math.log2(math.e)
SB_LOG2_UNDERFLOW = -150.0
VMEM_LIMIT = 60000 * 1024


def _rms(x, g):
    ms = jnp.mean(x * x, axis=-1, keepdims=True)
    return x * lax.rsqrt(ms + RMS_EPS) * g


def _dot(a, b):
    return jnp.dot(a, b, preferred_element_type=F32)


def _dot_nt(a, b):
    return lax.dot_general(a, b, (((1,), (1,)), ((), ())), preferred_element_type=F32)


def _resident(shape):
    nd = len(shape)
    return pl.BlockSpec(shape, lambda *_: (0,) * nd, pipeline_mode=pl.Buffered(1))


def _swiglu_into(acc_ref, h, w1_ref, w3_ref, w2_ref, before_chunk=None):
    d_ff = w1_ref.shape[1]
    for c in range(d_ff // FFN_CHUNK):
        for work in (before_chunk or {}).get(c, ()):
            work()
        cols = slice(c * FFN_CHUNK, (c + 1) * FFN_CHUNK)
        a = _dot(h, w1_ref[:, cols])
        b = _dot(h, w3_ref[:, cols])
        gated = (a * jax.nn.sigmoid(a) * b).astype(BF16)
        down = _dot(gated, w2_ref[cols, :])
        if c == 0:
            acc_ref[...] = down
        else:
            acc_ref[...] += down


def _pre_kernel(x_ref, g1_ref, w1_ref, w3_ref, w2_ref, gm_ref, wqkv_ref,
                x1_ref, qkv_ref, acc_ref):
    x = x_ref[...]
    _swiglu_into(acc_ref, _rms(x, g1_ref[...]).astype(BF16), w1_ref, w3_ref, w2_ref)
    x1 = x + 0.5 * acc_ref[...]
    x1_ref[...] = x1
    h = _rms(x1, gm_ref[...]).astype(BF16)
    n_qkv = qkv_ref.shape[1] // PROJ_CHUNK
    for c in range(n_qkv):
        cols = slice(c * PROJ_CHUNK, (c + 1) * PROJ_CHUNK)
        if c == n_qkv - 1 and n_qkv % 2:
            half = h.shape[0] // 2
            for rows in (slice(0, half), slice(half, None)):
                qkv_ref[rows, cols] = _dot(h[rows], wqkv_ref[:, cols]).astype(BF16)
        else:
            qkv_ref[:, cols] = _dot(h, wqkv_ref[:, cols]).astype(BF16)


def _pre(x, g1, w1, w3, w2, gm, wqkv):
    t, d = x.shape
    tm = PRE_TILE
    row = lambda i: (i, 0)
    return pl.pallas_call(
        _pre_kernel,
        grid=(t // tm,),
        in_specs=[pl.BlockSpec((tm, d), row), _resident(g1.shape), _resident(w1.shape),
                  _resident(w3.shape), _resident(w2.shape), _resident(gm.shape),
                  _resident(wqkv.shape)],
        out_specs=[pl.BlockSpec((tm, d), row), pl.BlockSpec((tm, wqkv.shape[1]), row)],
        out_shape=[jax.ShapeDtypeStruct((t, d), F32),
                   jax.ShapeDtypeStruct((t, wqkv.shape[1]), BF16)],
        scratch_shapes=[pltpu.VMEM((tm, d), F32)],
        compiler_params=pltpu.CompilerParams(
            dimension_semantics=("arbitrary",), vmem_limit_bytes=VMEM_LIMIT),
        name="pre",
    )(x, g1, w1, w3, w2, gm, wqkv)


def _post_kernel(sink_ref, x_ref, ob_ref, q_ref, kprev_ref, k_ref, vprev_ref, v_ref,
                 bias_ref, band_ref, gm_ref, wg_ref, wa_ref, wb_ref, wo_ref, g2_ref, w1_ref, w3_ref,
                 w2_ref, gf_ref, o_ref, acc_ref, mrg_ref, oa_ref, kwin_ref, vwin_ref, *,
                 final_norm, seq):
    s = pl.program_id(0)
    tm, d = x_ref.shape
    tile = jnp.minimum(s, pl.num_programs(0) - 2)
    first_rows = lax.rem(tile * tm, seq) == 0

    @pl.when(s == 0)
    def _():
        oa_ref[...] = jnp.zeros_like(oa_ref)

    attend = _swa_stages(sink_ref, q_ref, kprev_ref, k_ref, vprev_ref, v_ref, bias_ref, band_ref,
                         kwin_ref, vwin_ref, first_rows)
    logits = attend.logits()
    oa = oa_ref[...]
    ob = ob_ref[...]
    x1 = x_ref[...]
    h = _rms(x1, gm_ref[...]).astype(BF16)
    for c in range(d // PROJ_CHUNK):
        cols = slice(c * PROJ_CHUNK, (c + 1) * PROJ_CHUNK)
        gcols = slice(d + c * PROJ_CHUNK, d + (c + 1) * PROJ_CHUNK)
        merged = (jax.nn.sigmoid(_dot(h, wg_ref[:, cols])) * _dot(oa, wa_ref[:, cols])
                  + jax.nn.sigmoid(_dot(h, wg_ref[:, gcols])) * _dot(ob, wb_ref[:, cols]))
        mrg_ref[:, cols] = merged.astype(BF16)
    x2 = x1 + _dot(mrg_ref[...], wo_ref[...])
    spread = {}
    n_chunks = w1_ref.shape[1] // FFN_CHUNK
    stride = max(1, (n_chunks - 1 - SWA_FIRST_CHUNK - SWA_PV_LAG) // max(attend.blocks - 1, 1))
    for n in range(attend.blocks):
        first = SWA_FIRST_CHUNK + stride * n
        spread.setdefault(first, []).append(functools.partial(attend.softmax_block, n, logits))
        spread.setdefault(first + SWA_PV_LAG, []).append(
            functools.partial(attend.pv_block, n, oa_ref))
    _swiglu_into(acc_ref, _rms(x2, g2_ref[...]).astype(BF16), w1_ref, w3_ref, w2_ref,
                 before_chunk=spread)
    y = x2 + 0.5 * acc_ref[...]
    if final_norm:
        y = _rms(y, gf_ref[...])
    o_ref[...] = y


def _post(x, ob, qkv, sinks, bias, band, gm, wg, wa, wb, wo, g2, w1, w3, w2, gf, *, final_norm,
          seq, q_col, k_col, v_col):
    t, d = x.shape
    tm = POST_TILE
    n = t // tm
    qw = SWA_Q_HEADS * HEAD_DIM
    kvw = SWA_KV_HEADS * HEAD_DIM
    blocks_per_tile = tm // BLOCK
    done = lambda s: (jnp.maximum(s - 1, 0), 0)

    def ahead(col_block):
        return lambda s: (jnp.minimum(s, n - 1), col_block)

    def before(col_block):
        return lambda s: (jnp.maximum(jnp.minimum(s, n - 1) * blocks_per_tile - 1, 0), col_block)

    return pl.pallas_call(
        functools.partial(_post_kernel, final_norm=final_norm, seq=seq),
        grid=(n + 1,),
        in_specs=[pl.BlockSpec(memory_space=pltpu.SMEM),
                  pl.BlockSpec((tm, d), done), pl.BlockSpec((tm, ob.shape[1]), done),
                  pl.BlockSpec((tm, qw), ahead(q_col // qw)),
                  pl.BlockSpec((BLOCK, kvw), before(k_col // kvw)),
                  pl.BlockSpec((tm, kvw), ahead(k_col // kvw)),
                  pl.BlockSpec((BLOCK, kvw), before(v_col // kvw)),
                  pl.BlockSpec((tm, kvw), ahead(v_col // kvw)),
                  _resident(bias.shape), _resident(band.shape),
                  _resident(gm.shape), _resident(wg.shape),
                  _resident(wa.shape), _resident(wb.shape), _resident(wo.shape),
                  _resident(g2.shape), _resident(w1.shape), _resident(w3.shape),
                  _resident(w2.shape), _resident(gf.shape)],
        out_specs=pl.BlockSpec((tm, d), done),
        out_shape=jax.ShapeDtypeStruct((t, d), F32),
        scratch_shapes=[pltpu.VMEM((tm, d), F32), pltpu.VMEM((tm, d), BF16),
                        pltpu.VMEM((tm, qw), BF16),
                        pltpu.VMEM((tm + BLOCK, kvw), BF16), pltpu.VMEM((tm + BLOCK, kvw), BF16)],
        compiler_params=pltpu.CompilerParams(
            dimension_semantics=("arbitrary",), vmem_limit_bytes=VMEM_LIMIT),
        name="post",
    )(sinks, x, ob, qkv, qkv, qkv, qkv, qkv, bias, band, gm, wg, wa, wb, wo, g2, w1, w3, w2, gf)


class _swa_stages:
    def __init__(self, sink_ref, q_ref, kprev_ref, k_ref, vprev_ref, v_ref, bias_ref, band_ref,
                 kwin_ref, vwin_ref, first_rows):
        kwin_ref[0:BLOCK, :] = kprev_ref[...]
        kwin_ref[BLOCK:, :] = k_ref[...]
        vwin_ref[0:BLOCK, :] = vprev_ref[...]
        vwin_ref[BLOCK:, :] = v_ref[...]
        self.q_ref, self.kwin_ref, self.vwin_ref, self.bias_ref = q_ref, kwin_ref, vwin_ref, bias_ref
        self.blocks = q_ref.shape[0] // BLOCK
        lane = lax.broadcasted_iota(jnp.int32, (1, LANES), 1)
        self.low_half = lane < HEAD_DIM
        self.sels = (self.low_half, jnp.logical_not(self.low_half))
        key_col = lax.broadcasted_iota(jnp.int32, (1, 2 * BLOCK), 1)
        band = band_ref[...] > 0.0
        self.valid = [band] * self.blocks
        self.valid[0] = jnp.logical_and(
            band, jnp.logical_or(key_col >= BLOCK, jnp.logical_not(first_rows)))
        self.fills = [jnp.where(key_col == 0, sink_ref[hd], NEG_BIG) for hd in range(SWA_Q_HEADS)]
        not_row0 = lax.broadcasted_iota(jnp.int32, (2 * BLOCK, 1), 0) != 0
        self.vsels = tuple(jnp.logical_and(not_row0, sel) for sel in self.sels)
        self.probs = {}

    def logits(self):
        out = []
        for n in range(self.blocks):
            kw = self.kwin_ref[n * BLOCK:(n + 2) * BLOCK, :]
            for hd in range(SWA_Q_HEADS):
                p, e = divmod(hd, HEADS_PER_TILE)
                q = self.q_ref[n * BLOCK:(n + 1) * BLOCK, p * LANES:(p + 1) * LANES]
                out.append(_dot_nt(jnp.where(self.sels[e], q, jnp.zeros_like(q)), kw))
        return out

    def softmax_block(self, n, logits):
        self.probs[n] = []
        for hd in range(SWA_Q_HEADS):
            lg = jnp.where(self.valid[n], logits[n * SWA_Q_HEADS + hd] + self.bias_ref[hd],
                           self.fills[hd])
            pr = jnp.exp(lg - jnp.max(lg, axis=-1, keepdims=True))
            self.probs[n].append((pr.astype(BF16), 1.0 / jnp.sum(pr, axis=-1, keepdims=True)))

    def pv_block(self, n, o_ref):
        probs = self.probs[n]
        vw = self.vwin_ref[n * BLOCK:(n + 2) * BLOCK, :]
        vhalf = [jnp.where(sel, vw, jnp.zeros_like(vw)) for sel in self.vsels]
        for p in range(SWA_Q_HEADS // HEADS_PER_TILE):
            acc = jnp.zeros((BLOCK, LANES), F32)
            for e in range(HEADS_PER_TILE):
                acc = acc + _dot(probs[p * HEADS_PER_TILE + e][0], vhalf[e])
            inv = [probs[p * HEADS_PER_TILE + e][1] for e in range(HEADS_PER_TILE)]
            o_ref[n * BLOCK:(n + 1) * BLOCK, p * LANES:(p + 1) * LANES] = (
                acc * jnp.where(self.low_half, inv[0], inv[1])).astype(o_ref.dtype)


def _sb_kernel(q_ref, k_ref, v_ref, u_ref, o_ref):
    seq = q_ref.shape[0]
    half = SB_TQ // 2
    rows2 = HEADS_PER_TILE * SB_TQ
    top = slice(0, rows2 // 2)
    bottom = slice(rows2 // 2, rows2)
    lane = lax.broadcasted_iota(jnp.int32, (1, LANES), 1)
    low_half = lane < HEAD_DIM
    rows = lax.broadcasted_iota(jnp.int32, (SB_TQ, SB_UNIT), 0)
    cols = lax.broadcasted_iota(jnp.int32, (SB_TQ, SB_UNIT), 1)
    earlier = cols < rows
    causal = jnp.concatenate([earlier[:half]] * HEADS_PER_TILE + [earlier[half:]] * HEADS_PER_TILE,
                             axis=0)
    in_top = lax.broadcasted_iota(jnp.int32, (rows2, 1), 0) < rows2 // 2
    u = u_ref[...]

    def unit(x, i):
        return x[:, i * SB_UNIT:(i + 1) * SB_UNIT]

    def softplus_parts(w2):
        neg_abs = lax.bitcast_convert_type(
            lax.bitcast_convert_type(w2, jnp.uint32) | jnp.uint32(0x80000000), F32)
        log_keep = jnp.minimum(w2, 0.0) - jnp.log2(1.0 + jnp.exp2(neg_abs))
        hi = log_keep.astype(BF16)
        return hi, (log_keep - hi.astype(F32)).astype(BF16)

    def sweep(items):
        n = len(items)
        pieces, lhs, weights, runnings, out = [None] * n, [None] * n, [None] * n, [None] * n, [None] * n

        def logits_stage(t):
            q2, start, n_u, diag, upper_far, skip_top, _, _ = items[t]
            keys = k_ref[pl.ds(start, n_u * SB_UNIT), :]
            w2 = _dot_nt(q2, keys)
            cut = []
            for i in range(n_u):
                part = unit(w2, i)
                if diag and i == n_u - 1:
                    part = jnp.where(causal, part, -NEG_BIG)
                if skip_top:
                    part = jnp.where(in_top, -NEG_BIG, part)
                rws = top if (upper_far and i == 0) else slice(0, rows2)
                cut.append((i, rws, part[rws]))
            pieces[t] = cut
            split = [softplus_parts(part) for _, _, part in cut]
            lhs[t] = jnp.concatenate([jnp.concatenate([hi, lo], axis=1) for hi, lo in split], axis=0)

        def sums_stage(t):
            n_u, running = items[t][2], items[t][6]
            sums = _dot(lhs[t], u)
            offsets = np.cumsum([0] + [rws.stop - rws.start for _, rws, _ in pieces[t]])
            a = [None] * n_u
            for (i, rws, part), off in reversed(list(zip(pieces[t], offsets))):
                prod = sums[off:off + rws.stop - rws.start]
                local, total = unit(prod, 0), unit(prod, 1)
                c = local if running is None else local + running[rws]
                if rws == top:
                    running = jnp.concatenate([running[top] + total, running[bottom]], axis=0)
                    a[i] = jnp.concatenate([jnp.exp2(c - part).astype(BF16),
                                            jnp.zeros((rows2 // 2, SB_UNIT), BF16)], axis=0)
                else:
                    running = total if running is None else running + total
                    a[i] = jnp.exp2(c - part).astype(BF16)
            weights[t] = jnp.concatenate(a, axis=1)
            runnings[t] = running

        def output_stage(t):
            start, n_u, acc = items[t][1], items[t][2], items[t][7]
            both = _dot(weights[t], v_ref[pl.ds(start, n_u * SB_UNIT), :])
            mine = jnp.concatenate(
                [jnp.where(low_half, both[0:half], both[half:2 * half]),
                 jnp.where(low_half, both[2 * half:3 * half], both[3 * half:])], axis=0)
            out[t] = (runnings[t], mine if acc is None else acc + mine)

        for t in range(n + 2 * SB_STAGE_LAG):
            for stage, first in ((logits_stage, t), (sums_stage, t - SB_STAGE_LAG),
                                 (output_stage, t - 2 * SB_STAGE_LAG)):
                if 0 <= first < n:
                    stage(first)
        return out

    def stacked_q(i):
        q = q_ref[pl.ds(pl.multiple_of(i * SB_TQ, SB_TQ), SB_TQ), :]
        zero = jnp.zeros_like(q[:half])
        return jnp.concatenate(
            [jnp.where(low_half, q[:half], zero), jnp.where(low_half, zero, q[:half]),
             jnp.where(low_half, q[half:], zero), jnp.where(low_half, zero, q[half:])], axis=0)

    def alive(r):
        return jnp.max(r) > SB_LOG2_UNDERFLOW

    def finish(i, q2, running, acc):
        far = pl.multiple_of((i - (SB_WINDOW_UNITS - 1)) * SB_UNIT, SB_UNIT)
        (running, acc), = sweep([(q2, far, 1, False, False, True, running, acc)])

        def cond(state):
            return jnp.logical_and(state[0] >= 0, state[1])

        def body(state):
            ku, _, r, acc_ = state
            first = pl.multiple_of(ku * SB_UNIT, SB_UNIT)
            (r, acc_), = sweep([(q2, first, 1, False, False, False, r, acc_)])
            return ku - 1, alive(r), r, acc_

        return lax.while_loop(cond, body, (i - SB_WINDOW_UNITS, alive(running), running, acc))[3]

    def qblocks(idx):
        units = [min(i + 1, SB_WINDOW_UNITS) if isinstance(i, int) else SB_WINDOW_UNITS for i in idx]
        items = []
        for i, n_u in zip(idx, units):
            first = pl.multiple_of((i - (n_u - 1)) * SB_UNIT, SB_UNIT)
            items.append((stacked_q(i), first, n_u, True, n_u == SB_WINDOW_UNITS, False, None, None))
        swept = sweep(items)
        accs = [acc for _, acc in swept]
        full = [t for t, n_u in enumerate(units) if n_u == SB_WINDOW_UNITS]
        if full:
            worst = functools.reduce(jnp.maximum, [swept[t][0] for t in full])
            finished = lax.cond(
                alive(worst),
                lambda: [finish(idx[t], items[t][0], *swept[t]) for t in full],
                lambda: [accs[t] for t in full])
            for t, acc in zip(full, finished):
                accs[t] = acc
        for i, acc in zip(idx, accs):
            o_ref[pl.ds(pl.multiple_of(i * SB_TQ, SB_TQ), SB_TQ), :] = acc.astype(o_ref.dtype)

    n_q = seq // SB_TQ
    n_lead = SB_WINDOW_UNITS - 1
    n_groups = (n_q - n_lead) // SB_GROUP

    def step(j, carry):
        i = n_lead + SB_GROUP * j
        qblocks([i + g for g in range(SB_GROUP)])
        return carry

    lax.fori_loop(0, n_groups, step, 0)
    qblocks(list(range(n_lead)) + list(range(n_lead + n_groups * SB_GROUP, n_q)))


def _sb(qkv, u, *, batch, seq, q_col, k_col, v_col):
    t = qkv.shape[0]
    pairs = SB_HEADS // HEADS_PER_TILE

    def col(c0):
        return lambda b, p: (b, c0 // LANES + p)

    return pl.pallas_call(
        _sb_kernel,
        grid=(batch, pairs),
        in_specs=[
            pl.BlockSpec((seq, LANES), col(q_col)),
            pl.BlockSpec((seq, LANES), col(k_col)),
            pl.BlockSpec((seq, LANES), col(v_col)),
            _resident(u.shape),
        ],
        out_specs=pl.BlockSpec((seq, LANES), lambda b, p: (b, p)),
        out_shape=jax.ShapeDtypeStruct((t, SB_HEADS * HEAD_DIM), BF16),
        compiler_params=pltpu.CompilerParams(
            dimension_semantics=("arbitrary", "arbitrary"), vmem_limit_bytes=VMEM_LIMIT),
        name="sb",
    )(qkv, qkv, qkv, u)


def _rel_bucket(dist):
    max_exact = REL_BUCKETS // 2
    d = jnp.maximum(dist, 1).astype(F32)
    large = max_exact + (jnp.log(d / max_exact) / math.log(REL_MAX_DIST / max_exact)
                         * (REL_BUCKETS - max_exact)).astype(jnp.int32)
    large = jnp.minimum(large, REL_BUCKETS - 1)
    return jnp.where(dist < max_exact, dist, large)


def _swa_bias(rel_table):
    f = rel_table.astype(F32)[_rel_bucket(jnp.arange(SWA_WINDOW))].T
    heads = f.shape[0]
    span = 3 * BLOCK
    v = jnp.pad(f, ((0, 0), (BLOCK - 1, span - SWA_WINDOW - (BLOCK - 1))))
    shifted = jnp.tile(v, (1, BLOCK + 1))[:, :BLOCK * (span + 1)].reshape(heads, BLOCK, span + 1)
    return shifted[:, :, :2 * BLOCK][:, :, ::-1]


def _band_mask():
    qi = np.arange(BLOCK)[:, None] + BLOCK
    kj = np.arange(2 * BLOCK)[None, :]
    dist = qi - kj
    return jnp.asarray(((dist >= 0) & (dist < SWA_WINDOW)).astype(np.float32))


def _swa_head_order(w, axis, width=HEAD_DIM):
    shape = w.shape
    w = w.reshape(shape[:axis] + (SWA_KV_HEADS, SWA_Q_HEADS // SWA_KV_HEADS, width) + shape[axis + 1:])
    return jnp.swapaxes(w, axis, axis + 1).reshape(shape)


def kernel(x, norm_ffn1, ffn1_w1, ffn1_w3, ffn1_w2, norm_mix, w_in, swa_sinks, rel_bias,
           w_branch_swa, w_branch_sb, w_out, norm_ffn2, ffn2_w1, ffn2_w3, ffn2_w2, norm_final):
    batch, seq, d = x.shape
    depth = norm_ffn1.shape[0]
    assert seq % POST_TILE == 0 and (batch * seq) % PRE_TILE == 0, "token tiles must divide the sequence"
    assert seq // SB_TQ >= SB_WINDOW_UNITS and d % PROJ_CHUNK == 0 and ffn1_w1.shape[-1] % FFN_CHUNK == 0
    qa_w = SWA_Q_HEADS * HEAD_DIM
    kva_w = SWA_KV_HEADS * HEAD_DIM
    sb_w = SB_HEADS * HEAD_DIM
    scale = HEAD_DIM ** -0.5

    bias, band = _swa_head_order(_swa_bias(rel_bias), 0, 1), _band_mask()
    tri = (jnp.arange(SB_UNIT)[:, None] >= jnp.arange(SB_UNIT)[None, :]).astype(BF16)
    half = jnp.concatenate([tri, jnp.ones_like(tri)], axis=1)
    u = jnp.concatenate([half, half], axis=0)
    gain_final = norm_final.reshape(1, d)
    c_qa, c_ka, c_va = 0, qa_w, qa_w + kva_w
    c_qb = qa_w + 2 * kva_w
    c_kb, c_vb = c_qb + sb_w, c_qb + 2 * sb_w

    xt = x.reshape(batch * seq, d)
    for layer in range(depth):
        w = w_in[layer]
        o = 0
        cols = {}
        for name, width in (("qa", qa_w), ("ka", kva_w), ("va", kva_w), ("qb", sb_w), ("kb", sb_w),
                            ("vb", sb_w), ("g", 2 * d)):
            cols[name] = w[:, o:o + width]
            o += width
        wqkv = jnp.concatenate([_swa_head_order(cols["qa"], 1) * scale, cols["ka"], cols["va"],
                                cols["qb"] * (-scale * LOG2E), cols["kb"], cols["vb"]],
                               axis=1).astype(BF16)
        gain_mix = norm_mix[layer].reshape(1, d)
        x1, qkv = _pre(xt, norm_ffn1[layer].reshape(1, d), ffn1_w1[layer].astype(BF16),
                       ffn1_w3[layer].astype(BF16), ffn1_w2[layer].astype(BF16), gain_mix, wqkv)
        ob = _sb(qkv, u, batch=batch, seq=seq, q_col=c_qb, k_col=c_kb, v_col=c_vb)
        xt = _post(x1, ob, qkv, _swa_head_order(swa_sinks[layer], 0, 1), bias, band,
                   gain_mix, cols["g"].astype(BF16),
                   _swa_head_order(w_branch_swa[layer], 0).astype(BF16),
                   w_branch_sb[layer].astype(BF16),
                   w_out[layer].astype(BF16), norm_ffn2[layer].reshape(1, d),
                   ffn2_w1[layer].astype(BF16), ffn2_w3[layer].astype(BF16),
                   ffn2_w2[layer].astype(BF16), gain_final, final_norm=layer == depth - 1,
                   seq=seq, q_col=c_qa, k_col=c_ka, v_col=c_va)
    return xt.reshape(batch, seq, d)
```

```python
import functools
import math

import jax
import jax.numpy as jnp
import numpy as np
from jax import lax
from jax.experimental import pallas as pl
from jax.experimental.pallas import tpu as pltpu

F32 = jnp.float32
BF16 = jnp.bfloat16

HEAD_DIM = 64
SWA_Q_HEADS = 8
SWA_KV_HEADS = 2
SWA_WINDOW = 128
SB_HEADS = 8
BLOCK = 128
REL_BUCKETS = 32
REL_MAX_DIST = 128
RMS_EPS = 1e-6
NEG_BIG = -1e30

LANES = 128
HEADS_PER_TILE = LANES // HEAD_DIM
FFN_CHUNK = 256
PROJ_CHUNK = 256
SWA_FIRST_CHUNK = 1
SWA_PV_LAG = 2
PRE_TILE = 1024
POST_TILE = 512
SB_TQ = 128
SB_UNIT = 128
SB_WINDOW_UNITS = 3
SB_GROUP = 16
SB_STAGE_LAG = 1
LOG2E = math.log2(math.e)
SB_LOG2_UNDERFLOW = -150.0
VMEM_LIMIT = 60000 * 1024


def _rms(x, g):
    ms = jnp.mean(x * x, axis=-1, keepdims=True)
    return x * lax.rsqrt(ms + RMS_EPS) * g


def _dot(a, b):
    return jnp.dot(a, b, preferred_element_type=F32)


def _dot_nt(a, b):
    return lax.dot_general(a, b, (((1,), (1,)), ((), ())), preferred_element_type=F32)


def _resident(shape):
    nd = len(shape)
    return pl.BlockSpec(shape, lambda *_: (0,) * nd, pipeline_mode=pl.Buffered(1))


def _swiglu_into(acc_ref, h, w1_ref, w3_ref, w2_ref, before_chunk=None):
    d_ff = w1_ref.shape[1]
    for c in range(d_ff // FFN_CHUNK):
        for work in (before_chunk or {}).get(c, ()):
            work()
        cols = slice(c * FFN_CHUNK, (c + 1) * FFN_CHUNK)
        a = _dot(h, w1_ref[:, cols])
        b = _dot(h, w3_ref[:, cols])
        gated = (a * jax.nn.sigmoid(a) * b).astype(BF16)
        down = _dot(gated, w2_ref[cols, :])
        if c == 0:
            acc_ref[...] = down
        else:
            acc_ref[...] += down


def _pre_kernel(x_ref, g1_ref, w1_ref, w3_ref, w2_ref, gm_ref, wqkv_ref,
                x1_ref, qkv_ref, acc_ref):
    x = x_ref[...]
    _swiglu_into(acc_ref, _rms(x, g1_ref[...]).astype(BF16), w1_ref, w3_ref, w2_ref)
    x1 = x + 0.5 * acc_ref[...]
    x1_ref[...] = x1
    h = _rms(x1, gm_ref[...]).astype(BF16)
    n_qkv = qkv_ref.shape[1] // PROJ_CHUNK
    for c in range(n_qkv):
        cols = slice(c * PROJ_CHUNK, (c + 1) * PROJ_CHUNK)
        if c == n_qkv - 1 and n_qkv % 2:
            half = h.shape[0] // 2
            for rows in (slice(0, half), slice(half, None)):
                qkv_ref[rows, cols] = _dot(h[rows], wqkv_ref[:, cols]).astype(BF16)
        else:
            qkv_ref[:, cols] = _dot(h, wqkv_ref[:, cols]).astype(BF16)


def _pre(x, g1, w1, w3, w2, gm, wqkv):
    t, d = x.shape
    tm = PRE_TILE
    row = lambda i: (i, 0)
    return pl.pallas_call(
        _pre_kernel,
        grid=(t // tm,),
        in_specs=[pl.BlockSpec((tm, d), row), _resident(g1.shape), _resident(w1.shape),
                  _resident(w3.shape), _resident(w2.shape), _resident(gm.shape),
                  _resident(wqkv.shape)],
        out_specs=[pl.BlockSpec((tm, d), row), pl.BlockSpec((tm, wqkv.shape[1]), row)],
        out_shape=[jax.ShapeDtypeStruct((t, d), F32),
                   jax.ShapeDtypeStruct((t, wqkv.shape[1]), BF16)],
        scratch_shapes=[pltpu.VMEM((tm, d), F32)],
        compiler_params=pltpu.CompilerParams(
            dimension_semantics=("arbitrary",), vmem_limit_bytes=VMEM_LIMIT),
        name="pre",
    )(x, g1, w1, w3, w2, gm, wqkv)


def _post_kernel(sink_ref, x_ref, ob_ref, q_ref, kprev_ref, k_ref, vprev_ref, v_ref,
                 bias_ref, band_ref, gm_ref, wg_ref, wa_ref, wb_ref, wo_ref, g2_ref, w1_ref, w3_ref,
                 w2_ref, gf_ref, o_ref, acc_ref, mrg_ref, oa_ref, kwin_ref, vwin_ref, *,
                 final_norm, seq):
    s = pl.program_id(0)
    tm, d = x_ref.shape
    tile = jnp.minimum(s, pl.num_programs(0) - 2)
    first_rows = lax.rem(tile * tm, seq) == 0

    @pl.when(s == 0)
    def _():
        oa_ref[...] = jnp.zeros_like(oa_ref)

    attend = _swa_stages(sink_ref, q_ref, kprev_ref, k_ref, vprev_ref, v_ref, bias_ref, band_ref,
                         kwin_ref, vwin_ref, first_rows)
    logits = attend.logits()
    oa = oa_ref[...]
    ob = ob_ref[...]
    x1 = x_ref[...]
    h = _rms(x1, gm_ref[...]).astype(BF16)
    for c in range(d // PROJ_CHUNK):
        cols = slice(c * PROJ_CHUNK, (c + 1) * PROJ_CHUNK)
        gcols = slice(d + c * PROJ_CHUNK, d + (c + 1) * PROJ_CHUNK)
        merged = (jax.nn.sigmoid(_dot(h, wg_ref[:, cols])) * _dot(oa, wa_ref[:, cols])
                  + jax.nn.sigmoid(_dot(h, wg_ref[:, gcols])) * _dot(ob, wb_ref[:, cols]))
        mrg_ref[:, cols] = merged.astype(BF16)
    x2 = x1 + _dot(mrg_ref[...], wo_ref[...])
    spread = {}
    n_chunks = w1_ref.shape[1] // FFN_CHUNK
    stride = max(1, (n_chunks - 1 - SWA_FIRST_CHUNK - SWA_PV_LAG) // max(attend.blocks - 1, 1))
    for n in range(attend.blocks):
        first = SWA_FIRST_CHUNK + stride * n
        spread.setdefault(first, []).append(functools.partial(attend.softmax_block, n, logits))
        spread.setdefault(first + SWA_PV_LAG, []).append(
            functools.partial(attend.pv_block, n, oa_ref))
    _swiglu_into(acc_ref, _rms(x2, g2_ref[...]).astype(BF16), w1_ref, w3_ref, w2_ref,
                 before_chunk=spread)
    y = x2 + 0.5 * acc_ref[...]
    if final_norm:
        y = _rms(y, gf_ref[...])
    o_ref[...] = y


def _post(x, ob, qkv, sinks, bias, band, gm, wg, wa, wb, wo, g2, w1, w3, w2, gf, *, final_norm,
          seq, q_col, k_col, v_col):
    t, d = x.shape
    tm = POST_TILE
    n = t // tm
    qw = SWA_Q_HEADS * HEAD_DIM
    kvw = SWA_KV_HEADS * HEAD_DIM
    blocks_per_tile = tm // BLOCK
    done = lambda s: (jnp.maximum(s - 1, 0), 0)

    def ahead(col_block):
        return lambda s: (jnp.minimum(s, n - 1), col_block)

    def before(col_block):
        return lambda s: (jnp.maximum(jnp.minimum(s, n - 1) * blocks_per_tile - 1, 0), col_block)

    return pl.pallas_call(
        functools.partial(_post_kernel, final_norm=final_norm, seq=seq),
        grid=(n + 1,),
        in_specs=[pl.BlockSpec(memory_space=pltpu.SMEM),
                  pl.BlockSpec((tm, d), done), pl.BlockSpec((tm, ob.shape[1]), done),
                  pl.BlockSpec((tm, qw), ahead(q_col // qw)),
                  pl.BlockSpec((BLOCK, kvw), before(k_col // kvw)),
                  pl.BlockSpec((tm, kvw), ahead(k_col // kvw)),
                  pl.BlockSpec((BLOCK, kvw), before(v_col // kvw)),
                  pl.BlockSpec((tm, kvw), ahead(v_col // kvw)),
                  _resident(bias.shape), _resident(band.shape),
                  _resident(gm.shape), _resident(wg.shape),
                  _resident(wa.shape), _resident(wb.shape), _resident(wo.shape),
                  _resident(g2.shape), _resident(w1.shape), _resident(w3.shape),
                  _resident(w2.shape), _resident(gf.shape)],
        out_specs=pl.BlockSpec((tm, d), done),
        out_shape=jax.ShapeDtypeStruct((t, d), F32),
        scratch_shapes=[pltpu.VMEM((tm, d), F32), pltpu.VMEM((tm, d), BF16),
                        pltpu.VMEM((tm, qw), BF16),
                        pltpu.VMEM((tm + BLOCK, kvw), BF16), pltpu.VMEM((tm + BLOCK, kvw), BF16)],
        compiler_params=pltpu.CompilerParams(
            dimension_semantics=("arbitrary",), vmem_limit_bytes=VMEM_LIMIT),
        name="post",
    )(sinks, x, ob, qkv, qkv, qkv, qkv, qkv, bias, band, gm, wg, wa, wb, wo, g2, w1, w3, w2, gf)


class _swa_stages:
    def __init__(self, sink_ref, q_ref, kprev_ref, k_ref, vprev_ref, v_ref, bias_ref, band_ref,
                 kwin_ref, vwin_ref, first_rows):
        kwin_ref[0:BLOCK, :] = kprev_ref[...]
        kwin_ref[BLOCK:, :] = k_ref[...]
        vwin_ref[0:BLOCK, :] = vprev_ref[...]
        vwin_ref[BLOCK:, :] = v_ref[...]
        self.q_ref, self.kwin_ref, self.vwin_ref, self.bias_ref = q_ref, kwin_ref, vwin_ref, bias_ref
        self.blocks = q_ref.shape[0] // BLOCK
        lane = lax.broadcasted_iota(jnp.int32, (1, LANES), 1)
        self.low_half = lane < HEAD_DIM
        self.sels = (self.low_half, jnp.logical_not(self.low_half))
        key_col = lax.broadcasted_iota(jnp.int32, (1, 2 * BLOCK), 1)
        band = band_ref[...] > 0.0
        self.valid = [band] * self.blocks
        self.valid[0] = jnp.logical_and(
            band, jnp.logical_or(key_col >= BLOCK, jnp.logical_not(first_rows)))
        self.fills = [jnp.where(key_col == 0, sink_ref[hd], NEG_BIG) for hd in range(SWA_Q_HEADS)]
        not_row0 = lax.broadcasted_iota(jnp.int32, (2 * BLOCK, 1), 0) != 0
        self.vsels = tuple(jnp.logical_and(not_row0, sel) for sel in self.sels)
        self.probs = {}

    def logits(self):
        out = []
        for n in range(self.blocks):
            kw = self.kwin_ref[n * BLOCK:(n + 2) * BLOCK, :]
            for hd in range(SWA_Q_HEADS):
                p, e = divmod(hd, HEADS_PER_TILE)
                q = self.q_ref[n * BLOCK:(n + 1) * BLOCK, p * LANES:(p + 1) * LANES]
                out.append(_dot_nt(jnp.where(self.sels[e], q, jnp.zeros_like(q)), kw))
        return out

    def softmax_block(self, n, logits):
        self.probs[n] = []
        for hd in range(SWA_Q_HEADS):
            lg = jnp.where(self.valid[n], logits[n * SWA_Q_HEADS + hd] + self.bias_ref[hd],
                           self.fills[hd])
            pr = jnp.exp(lg - jnp.max(lg, axis=-1, keepdims=True))
            self.probs[n].append((pr.astype(BF16), 1.0 / jnp.sum(pr, axis=-1, keepdims=True)))

    def pv_block(self, n, o_ref):
        probs = self.probs[n]
        vw = self.vwin_ref[n * BLOCK:(n + 2) * BLOCK, :]
        vhalf = [jnp.where(sel, vw, jnp.zeros_like(vw)) for sel in self.vsels]
        for p in range(SWA_Q_HEADS // HEADS_PER_TILE):
            acc = jnp.zeros((BLOCK, LANES), F32)
            for e in range(HEADS_PER_TILE):
                acc = acc + _dot(probs[p * HEADS_PER_TILE + e][0], vhalf[e])
            inv = [probs[p * HEADS_PER_TILE + e][1] for e in range(HEADS_PER_TILE)]
            o_ref[n * BLOCK:(n + 1) * BLOCK, p * LANES:(p + 1) * LANES] = (
                acc * jnp.where(self.low_half, inv[0], inv[1])).astype(o_ref.dtype)


def _sb_kernel(q_ref, k_ref, v_ref, u_ref, o_ref):
    seq = q_ref.shape[0]
    half = SB_TQ // 2
    rows2 = HEADS_PER_TILE * SB_TQ
    top = slice(0, rows2 // 2)
    bottom = slice(rows2 // 2, rows2)
    lane = lax.broadcasted_iota(jnp.int32, (1, LANES), 1)
    low_half = lane < HEAD_DIM
    rows = lax.broadcasted_iota(jnp.int32, (SB_TQ, SB_UNIT), 0)
    cols = lax.broadcasted_iota(jnp.int32, (SB_TQ, SB_UNIT), 1)
    earlier = cols < rows
    causal = jnp.concatenate([earlier[:half]] * HEADS_PER_TILE + [earlier[half:]] * HEADS_PER_TILE,
                             axis=0)
    in_top = lax.broadcasted_iota(jnp.int32, (rows2, 1), 0) < rows2 // 2
    u = u_ref[...]

    def unit(x, i):
        return x[:, i * SB_UNIT:(i + 1) * SB_UNIT]

    def softplus_parts(w2):
        neg_abs = lax.bitcast_convert_type(
            lax.bitcast_convert_type(w2, jnp.uint32) | jnp.uint32(0x80000000), F32)
        log_keep = jnp.minimum(w2, 0.0) - jnp.log2(1.0 + jnp.exp2(neg_abs))
        hi = log_keep.astype(BF16)
        return hi, (log_keep - hi.astype(F32)).astype(BF16)

    def sweep(items):
        n = len(items)
        pieces, lhs, weights, runnings, out = [None] * n, [None] * n, [None] * n, [None] * n, [None] * n

        def logits_stage(t):
            q2, start, n_u, diag, upper_far, skip_top, _, _ = items[t]
            keys = k_ref[pl.ds(start, n_u * SB_UNIT), :]
            if upper_far:
                near = _dot_nt(q2[bottom], keys[SB_UNIT:])
                w2 = jnp.concatenate(
                    [_dot_nt(q2[top], keys),
                     jnp.concatenate([jnp.zeros((rows2 // 2, SB_UNIT), F32), near], axis=1)], axis=0)
            else:
                w2 = _dot_nt(q2, keys)
            cut = []
            for i in range(n_u):
                part = unit(w2, i)
                if diag and i == n_u - 1:
                    part = jnp.where(causal, part, -NEG_BIG)
                if skip_top:
                    part = jnp.where(in_top, -NEG_BIG, part)
                rws = top if (upper_far and i == 0) else slice(0, rows2)
                cut.append((i, rws, part[rws]))
            pieces[t] = cut
            split = [softplus_parts(part) for _, _, part in cut]
            lhs[t] = jnp.concatenate([jnp.concatenate([hi, lo], axis=1) for hi, lo in split], axis=0)

        def sums_stage(t):
            n_u, running = items[t][2], items[t][6]
            sums = _dot(lhs[t], u)
            offsets = np.cumsum([0] + [rws.stop - rws.start for _, rws, _ in pieces[t]])
            a = [None] * n_u
            for (i, rws, part), off in reversed(list(zip(pieces[t], offsets))):
                prod = sums[off:off + rws.stop - rws.start]
                local, total = unit(prod, 0), unit(prod, 1)
                c = local if running is None else local + running[rws]
                if rws == top:
                    running = jnp.concatenate([running[top] + total, running[bottom]], axis=0)
                    a[i] = jnp.concatenate([jnp.exp2(c - part).astype(BF16),
                                            jnp.zeros((rows2 // 2, SB_UNIT), BF16)], axis=0)
                else:
                    running = total if running is None else running + total
                    a[i] = jnp.exp2(c - part).astype(BF16)
            weights[t] = jnp.concatenate(a, axis=1)
            runnings[t] = running

        def output_stage(t):
            start, n_u, upper_far, acc = items[t][1], items[t][2], items[t][4], items[t][7]
            values = v_ref[pl.ds(start, n_u * SB_UNIT), :]
            if upper_far:
                both = jnp.concatenate([_dot(weights[t][top], values),
                                        _dot(weights[t][bottom, SB_UNIT:], values[SB_UNIT:])], axis=0)
            else:
                both = _dot(weights[t], values)
            mine = jnp.concatenate(
                [jnp.where(low_half, both[0:half], both[half:2 * half]),
                 jnp.where(low_half, both[2 * half:3 * half], both[3 * half:])], axis=0)
            out[t] = (runnings[t], mine if acc is None else acc + mine)

        for t in range(n + 2 * SB_STAGE_LAG):
            for stage, first in ((logits_stage, t), (sums_stage, t - SB_STAGE_LAG),
                                 (output_stage, t - 2 * SB_STAGE_LAG)):
                if 0 <= first < n:
                    stage(first)
        return out

    def stacked_q(i):
        q = q_ref[pl.ds(pl.multiple_of(i * SB_TQ, SB_TQ), SB_TQ), :]
        zero = jnp.zeros_like(q[:half])
        return jnp.concatenate(
            [jnp.where(low_half, q[:half], zero), jnp.where(low_half, zero, q[:half]),
             jnp.where(low_half, q[half:], zero), jnp.where(low_half, zero, q[half:])], axis=0)

    def alive(r):
        return jnp.max(r) > SB_LOG2_UNDERFLOW

    def finish(i, q2, running, acc):
        far = pl.multiple_of((i - (SB_WINDOW_UNITS - 1)) * SB_UNIT, SB_UNIT)
        (running, acc), = sweep([(q2, far, 1, False, False, True, running, acc)])

        def cond(state):
            return jnp.logical_and(state[0] >= 0, state[1])

        def body(state):
            ku, _, r, acc_ = state
            first = pl.multiple_of(ku * SB_UNIT, SB_UNIT)
            (r, acc_), = sweep([(q2, first, 1, False, False, False, r, acc_)])
            return ku - 1, alive(r), r, acc_

        return lax.while_loop(cond, body, (i - SB_WINDOW_UNITS, alive(running), running, acc))[3]

    def qblocks(idx):
        units = [min(i + 1, SB_WINDOW_UNITS) if isinstance(i, int) else SB_WINDOW_UNITS for i in idx]
        items = []
        for i, n_u in zip(idx, units):
            first = pl.multiple_of((i - (n_u - 1)) * SB_UNIT, SB_UNIT)
            items.append((stacked_q(i), first, n_u, True, n_u == SB_WINDOW_UNITS, False, None, None))
        swept = sweep(items)
        accs = [acc for _, acc in swept]
        full = [t for t, n_u in enumerate(units) if n_u == SB_WINDOW_UNITS]
        if full:
            worst = functools.reduce(jnp.maximum, [swept[t][0] for t in full])
            finished = lax.cond(
                alive(worst),
                lambda: [finish(idx[t], items[t][0], *swept[t]) for t in full],
                lambda: [accs[t] for t in full])
            for t, acc in zip(full, finished):
                accs[t] = acc
        for i, acc in zip(idx, accs):
            o_ref[pl.ds(pl.multiple_of(i * SB_TQ, SB_TQ), SB_TQ), :] = acc.astype(o_ref.dtype)

    n_q = seq // SB_TQ
    n_lead = SB_WINDOW_UNITS - 1
    n_groups = (n_q - n_lead) // SB_GROUP

    def step(j, carry):
        i = n_lead + SB_GROUP * j
        qblocks([i + g for g in range(SB_GROUP)])
        return carry

    lax.fori_loop(0, n_groups, step, 0)
    qblocks(list(range(n_lead)) + list(range(n_lead + n_groups * SB_GROUP, n_q)))


def _sb(qkv, u, *, batch, seq, q_col, k_col, v_col):
    t = qkv.shape[0]
    pairs = SB_HEADS // HEADS_PER_TILE

    def col(c0):
        return lambda b, p: (b, c0 // LANES + p)

    return pl.pallas_call(
        _sb_kernel,
        grid=(batch, pairs),
        in_specs=[
            pl.BlockSpec((seq, LANES), col(q_col)),
            pl.BlockSpec((seq, LANES), col(k_col)),
            pl.BlockSpec((seq, LANES), col(v_col)),
            _resident(u.shape),
        ],
        out_specs=pl.BlockSpec((seq, LANES), lambda b, p: (b, p)),
        out_shape=jax.ShapeDtypeStruct((t, SB_HEADS * HEAD_DIM), BF16),
        compiler_params=pltpu.CompilerParams(
            dimension_semantics=("arbitrary", "arbitrary"), vmem_limit_bytes=VMEM_LIMIT),
        name="sb",
    )(qkv, qkv, qkv, u)


def _rel_bucket(dist):
    max_exact = REL_BUCKETS // 2
    d = jnp.maximum(dist, 1).astype(F32)
    large = max_exact + (jnp.log(d / max_exact) / math.log(REL_MAX_DIST / max_exact)
                         * (REL_BUCKETS - max_exact)).astype(jnp.int32)
    large = jnp.minimum(large, REL_BUCKETS - 1)
    return jnp.where(dist < max_exact, dist, large)


def _swa_bias(rel_table):
    f = rel_table.astype(F32)[_rel_bucket(jnp.arange(SWA_WINDOW))].T
    heads = f.shape[0]
    span = 3 * BLOCK
    v = jnp.pad(f, ((0, 0), (BLOCK - 1, span - SWA_WINDOW - (BLOCK - 1))))
    shifted = jnp.tile(v, (1, BLOCK + 1))[:, :BLOCK * (span + 1)].reshape(heads, BLOCK, span + 1)
    return shifted[:, :, :2 * BLOCK][:, :, ::-1]


def _band_mask():
    qi = np.arange(BLOCK)[:, None] + BLOCK
    kj = np.arange(2 * BLOCK)[None, :]
    dist = qi - kj
    return jnp.asarray(((dist >= 0) & (dist < SWA_WINDOW)).astype(np.float32))


def _swa_head_order(w, axis, width=HEAD_DIM):
    shape = w.shape
    w = w.reshape(shape[:axis] + (SWA_KV_HEADS, SWA_Q_HEADS // SWA_KV_HEADS, width) + shape[axis + 1:])
    return jnp.swapaxes(w, axis, axis + 1).reshape(shape)


def kernel(x, norm_ffn1, ffn1_w1, ffn1_w3, ffn1_w2, norm_mix, w_in, swa_sinks, rel_bias,
           w_branch_swa, w_branch_sb, w_out, norm_ffn2, ffn2_w1, ffn2_w3, ffn2_w2, norm_final):
    batch, seq, d = x.shape
    depth = norm_ffn1.shape[0]
    assert seq % POST_TILE == 0 and (batch * seq) % PRE_TILE == 0, "token tiles must divide the sequence"
    assert seq // SB_TQ >= SB_WINDOW_UNITS and d % PROJ_CHUNK == 0 and ffn1_w1.shape[-1] % FFN_CHUNK == 0
    qa_w = SWA_Q_HEADS * HEAD_DIM
    kva_w = SWA_KV_HEADS * HEAD_DIM
    sb_w = SB_HEADS * HEAD_DIM
    scale = HEAD_DIM ** -0.5

    bias, band = _swa_head_order(_swa_bias(rel_bias), 0, 1), _band_mask()
    tri = (jnp.arange(SB_UNIT)[:, None] >= jnp.arange(SB_UNIT)[None, :]).astype(BF16)
    half = jnp.concatenate([tri, jnp.ones_like(tri)], axis=1)
    u = jnp.concatenate([half, half], axis=0)
    gain_final = norm_final.reshape(1, d)
    c_qa, c_ka, c_va = 0, qa_w, qa_w + kva_w
    c_qb = qa_w + 2 * kva_w
    c_kb, c_vb = c_qb + sb_w, c_qb + 2 * sb_w

    xt = x.reshape(batch * seq, d)
    for layer in range(depth):
        w = w_in[layer]
        o = 0
        cols = {}
        for name, width in (("qa", qa_w), ("ka", kva_w), ("va", kva_w), ("qb", sb_w), ("kb", sb_w),
                            ("vb", sb_w), ("g", 2 * d)):
            cols[name] = w[:, o:o + width]
            o += width
        wqkv = jnp.concatenate([_swa_head_order(cols["qa"], 1) * scale, cols["ka"], cols["va"],
                                cols["qb"] * (-scale * LOG2E), cols["kb"], cols["vb"]],
                               axis=1).astype(BF16)
        gain_mix = norm_mix[layer].reshape(1, d)
        x1, qkv = _pre(xt, norm_ffn1[layer].reshape(1, d), ffn1_w1[layer].astype(BF16),
                       ffn1_w3[layer].astype(BF16), ffn1_w2[layer].astype(BF16), gain_mix, wqkv)
        ob = _sb(qkv, u, batch=batch, seq=seq, q_col=c_qb, k_col=c_kb, v_col=c_vb)
        xt = _post(x1, ob, qkv, _swa_head_order(swa_sinks[layer], 0, 1), bias, band,
                   gain_mix, cols["g"].astype(BF16),
                   _swa_head_order(w_branch_swa[layer], 0).astype(BF16),
                   w_branch_sb[layer].astype(BF16),
                   w_out[layer].astype(BF16), norm_ffn2[layer].reshape(1, d),
                   ffn2_w1[layer].astype(BF16), ffn2_w3[layer].astype(BF16),
                   ffn2_w2[layer].astype(BF16), gain_final, final_norm=layer == depth - 1,
                   seq=seq, q_col=c_qa, k_col=c_ka, v_col=c_va)
    return xt.reshape(batch, seq, d)
```
